```python
import math
import jax
import jax.numpy as jnp
from jax import lax
import numpy as np

D_MODEL = 1024
BATCH = 8
SEQ = 8192
DEPTH = 1

HEAD_DIM = 64
MEM_LEN = 256
GRID_W = 64
NA_HEADS = D_MODEL // (2 * HEAD_DIM)
NA_WIN_ROWS = 8
NA_WIN_COLS = 16
DIL_PAIRS = ((128, 1), (512, 4), (2048, 16))
DIL_GROUPS = len(DIL_PAIRS)
DIL_HEADS_PER_GROUP = D_MODEL // (4 * HEAD_DIM)
DIL_HEADS = DIL_GROUPS * DIL_HEADS_PER_GROUP
DIL_BLOCK = 128
MEM_HEADS = 4
T5_BUCKETS = 32
T5_MAX_DIST = 1024
N_GROUPS = 4
EXPERTS_PER_GROUP = 8
N_EXPERTS = N_GROUPS * EXPERTS_PER_GROUP
TOP_K = 2
D_EXPERT = D_MODEL // 2
MOE_BLOCK = 256
EPS = 1e-6
NEG_INF = -1e30

NA_W = NA_HEADS * HEAD_DIM
DIL_W = DIL_HEADS * HEAD_DIM
DIL_OUT_W = DIL_HEADS_PER_GROUP * HEAD_DIM
MEM_W = MEM_HEADS * HEAD_DIM
IN_COLS = 3 * NA_W + 3 * DIL_W + MEM_W
MIX_W = NA_W + DIL_OUT_W + MEM_W
IN_SPLITS = (NA_W, 2 * NA_W, 3 * NA_W, 3 * NA_W + DIL_W, 3 * NA_W + 2 * DIL_W, 3 * NA_W + 3 * DIL_W)

kernel_name = 'hybrid_natten_dilated_memory_hmoe_encoder'


def rms_norm(x, g):
    xf = x.astype(jnp.float32)
    y = xf * lax.rsqrt(jnp.mean(xf * xf, axis=-1, keepdims=True) + EPS)
    return (y * g.astype(jnp.float32)).astype(x.dtype)


def t5_bucket(rel):
    nb = T5_BUCKETS // 2
    max_exact = nb // 2
    n = jnp.abs(rel)
    upper = (rel > 0).astype(jnp.int32) * nb
    nf = jnp.maximum(n, 1).astype(jnp.float32)
    large = max_exact + (jnp.log(nf / max_exact) / math.log(T5_MAX_DIST / max_exact) * (nb - max_exact)).astype(jnp.int32)
    large = jnp.minimum(large, nb - 1)
    return upper + jnp.where(n < max_exact, n, large)


def neighborhood_attention(q, k, v, rpb):
    b, s, h, dh = q.shape
    rows = s // GRID_W
    kh = min(NA_WIN_ROWS, rows)
    kw = NA_WIN_COLS
    kg = k.reshape(b, rows, GRID_W, h, dh).transpose(0, 3, 1, 2, 4)
    vg = v.reshape(b, rows, GRID_W, h, dh).transpose(0, 3, 1, 2, 4)
    qg = q.reshape(b, rows, GRID_W, h, dh).transpose(1, 0, 3, 2, 4)
    cols = jnp.arange(GRID_W)
    c0 = jnp.clip(cols - kw // 2, 0, GRID_W - kw)
    col_idx = c0[:, None] + jnp.arange(kw)[None, :]
    dc = col_idx - cols[:, None] + (NA_WIN_COLS - 1)
    scale = HEAD_DIM ** -0.5

    def row_fn(args):
        q_r, r = args
        r0 = jnp.clip(r - kh // 2, 0, rows - kh)
        k_rows = lax.dynamic_slice_in_dim(kg, r0, kh, axis=2)
        v_rows = lax.dynamic_slice_in_dim(vg, r0, kh, axis=2)
        k_win = k_rows[:, :, :, col_idx]
        v_win = v_rows[:, :, :, col_idx]
        dr = r0 + jnp.arange(kh) - r + (NA_WIN_ROWS - 1)
        bias = rpb[:, dr[:, None, None], dc[None]].transpose(0, 2, 1, 3)
        logits = jnp.einsum('bhcd,bhacjd->bhcaj', q_r, k_win).astype(jnp.float32) * scale + bias.astype(jnp.float32)[None]
        p = jax.nn.softmax(logits, axis=(-2, -1)).astype(v.dtype)
        return jnp.einsum('bhcaj,bhacjd->bhcd', p, v_win)

    out = lax.map(row_fn, (qg, jnp.arange(rows)))
    return out.transpose(1, 0, 3, 2, 4).reshape(b, s, h * dh)


def dilated_window_attention(q, k, v, offs, bias, half):
    b, s, h, dh = q.shape
    scale = HEAD_DIM ** -0.5
    kp = jnp.pad(k, ((0, 0), (half, half), (0, 0), (0, 0)))
    vp = jnp.pad(v, ((0, 0), (half, half), (0, 0), (0, 0)))
    local = jnp.arange(DIL_BLOCK)[:, None] + offs[None, :] + half
    bias32 = bias.astype(jnp.float32)
    n_blk = s // DIL_BLOCK

    def block_fn(i):
        s0 = i * DIL_BLOCK
        qb = lax.dynamic_slice_in_dim(q, s0, DIL_BLOCK, axis=1)
        kb = lax.dynamic_slice_in_dim(kp, s0, DIL_BLOCK + 2 * half, axis=1)
        vb = lax.dynamic_slice_in_dim(vp, s0, DIL_BLOCK + 2 * half, axis=1)
        kgat = kb[:, local]
        vgat = vb[:, local]
        key_pos = s0 + local - half
        valid = (key_pos >= 0) & (key_pos < s)
        logits = jnp.einsum('bqhd,bqjhd->bqhj', qb, kgat).astype(jnp.float32) * scale + bias32[None, None]
        logits = jnp.where(valid[None, :, None, :], logits, NEG_INF)
        m = jnp.max(logits, axis=-1)
        e = jnp.exp(logits - m[..., None])
        den = jnp.sum(e, axis=-1)
        o = jnp.einsum('bqhj,bqjhd->bqhd', e, vgat.astype(jnp.float32)) / den[..., None]
        return o, m, den

    o, m, den = lax.map(block_fn, jnp.arange(n_blk))
    o = o.transpose(1, 0, 2, 3, 4).reshape(b, s, h, dh)
    m = m.transpose(1, 0, 2, 3).reshape(b, s, h)
    den = den.transpose(1, 0, 2, 3).reshape(b, s, h)
    return o, m, den


def hierarchical_moe(h, w_r1, b_r1, w_r2, b_r2, w1, w3, w2):
    b, s, d = h.shape
    n = b * s
    t = h.reshape(n, d)
    grp_logits = (t @ w_r1).astype(jnp.float32) + b_r1.astype(jnp.float32)
    grp_prob = jax.nn.softmax(grp_logits, axis=-1)
    grp_idx = jnp.argmax(grp_logits, axis=-1)
    grp_gate = jnp.take_along_axis(grp_prob, grp_idx[:, None], axis=-1)
    fine_logits = ((t @ w_r2).astype(jnp.float32) + b_r2.astype(jnp.float32)).reshape(n, N_GROUPS, EXPERTS_PER_GROUP)
    fine_sel = jnp.take_along_axis(fine_logits, grp_idx[:, None, None], axis=1)[:, 0]
    top_val, top_idx = lax.top_k(fine_sel, TOP_K)
    gate = grp_gate * jax.nn.softmax(top_val, axis=-1)
    exp_idx = grp_idx[:, None] * EXPERTS_PER_GROUP + top_idx
    n_rows = n * TOP_K
    e_flat = exp_idx.reshape(-1)
    g_flat = gate.reshape(-1)
    tok_flat = jnp.repeat(jnp.arange(n), TOP_K)
    order = jnp.argsort(e_flat)
    e_sorted = e_flat[order]
    counts = jnp.bincount(e_flat, length=N_EXPERTS)
    cnt_start = jnp.cumsum(counts) - counts
    padded = (counts + MOE_BLOCK - 1) // MOE_BLOCK * MOE_BLOCK
    pad_end = jnp.cumsum(padded)
    pad_start = pad_end - padded
    dest = pad_start[e_sorted] + jnp.arange(n_rows) - cnt_start[e_sorted]
    cap = n_rows + N_EXPERTS * MOE_BLOCK
    row_tok = jnp.full((cap,), n, jnp.int32).at[dest].set(tok_flat[order])
    row_gate = jnp.zeros((cap,), jnp.float32).at[dest].set(g_flat[order])
    n_blk = cap // MOE_BLOCK
    blk_exp = jnp.minimum(jnp.searchsorted(pad_end, jnp.arange(n_blk) * MOE_BLOCK, side='right'), N_EXPERTS - 1)
    t_pad = jnp.concatenate([t, jnp.zeros((1, d), t.dtype)], axis=0)
    xs = t_pad[row_tok].reshape(n_blk, MOE_BLOCK, d)

    def expert_block(args):
        xb, e = args
        a = jax.nn.silu(xb @ w1[e]) * (xb @ w3[e])
        return a @ w2[e]

    yb = lax.map(expert_block, (xs, blk_exp)).reshape(cap, d)
    y = jnp.zeros((n + 1, d), h.dtype).at[row_tok].add(yb * row_gate[:, None].astype(h.dtype))
    return y[:n].reshape(b, s, d)


def setup_inputs(seed: int = 0) -> dict:
    key = jax.random.key(seed)
    ks = jax.random.split(key, 18)
    f32 = jnp.float32

    def nrm(k, shape, sc):
        return jax.random.normal(k, shape, f32) * sc

    L = DEPTH
    return {
        'x': nrm(ks[0], (BATCH, SEQ, D_MODEL), 1.0),
        'mem': nrm(ks[1], (BATCH, MEM_LEN, D_MODEL), 1.0),
        'g_mix': 1.0 + nrm(ks[2], (L, D_MODEL), 0.02),
        'w_in': nrm(ks[3], (L, D_MODEL, IN_COLS), D_MODEL ** -0.5),
        'qk_gain': 1.0 + nrm(ks[4], (L, 3, 2, HEAD_DIM), 0.02),
        'na_rpb': nrm(ks[5], (L, NA_HEADS, 2 * NA_WIN_ROWS - 1, 2 * NA_WIN_COLS - 1), 0.1),
        't5_table': nrm(ks[6], (T5_BUCKETS, DIL_HEADS), 0.1),
        'g_mem': 1.0 + nrm(ks[7], (L, D_MODEL), 0.02),
        'w_mem_kv': nrm(ks[8], (L, D_MODEL, 2 * MEM_W), D_MODEL ** -0.5),
        'w_out': nrm(ks[9], (L, MIX_W, D_MODEL), MIX_W ** -0.5),
        'g_ffn': 1.0 + nrm(ks[10], (L, D_MODEL), 0.02),
        'w_r1': nrm(ks[11], (L, D_MODEL, N_GROUPS), D_MODEL ** -0.5),
        'b_r1': nrm(ks[12], (L, N_GROUPS), 0.01),
        'w_r2': nrm(ks[13], (L, D_MODEL, N_EXPERTS), D_MODEL ** -0.5),
        'b_r2': nrm(ks[14], (L, N_EXPERTS), 0.01),
        'w1': nrm(ks[15], (L, N_EXPERTS, D_MODEL, D_EXPERT), D_MODEL ** -0.5),
        'w3': nrm(ks[16], (L, N_EXPERTS, D_MODEL, D_EXPERT), D_MODEL ** -0.5),
        'w2': nrm(ks[17], (L, N_EXPERTS, D_EXPERT, D_MODEL), D_EXPERT ** -0.5),
    }


def reference(x, mem, g_mix, w_in, qk_gain, na_rpb, t5_table, g_mem, w_mem_kv, w_out, g_ffn, w_r1, b_r1, w_r2, b_r2, w1, w3, w2):
    b, s, _ = x.shape
    hd = HEAD_DIM
    scale = HEAD_DIM ** -0.5
    t5 = t5_table.reshape(T5_BUCKETS, DIL_GROUPS, DIL_HEADS_PER_GROUP)
    for l in range(DEPTH):
        h = rms_norm(x, g_mix[l])
        qa, ka, va, qd, kd, vd, qm = jnp.split(h @ w_in[l], IN_SPLITS, axis=-1)

        qa = rms_norm(qa.reshape(b, s, NA_HEADS, hd), qk_gain[l, 0, 0])
        ka = rms_norm(ka.reshape(b, s, NA_HEADS, hd), qk_gain[l, 0, 1])
        va = va.reshape(b, s, NA_HEADS, hd)
        out_na = neighborhood_attention(qa, ka, va, na_rpb[l])

        dshape = (b, s, DIL_GROUPS, DIL_HEADS_PER_GROUP, hd)
        qd = rms_norm(qd.reshape(dshape), qk_gain[l, 1, 0])
        kd = rms_norm(kd.reshape(dshape), qk_gain[l, 1, 1])
        vd = vd.reshape(dshape)
        o_list, m_list, d_list = [], [], []
        for g, (win, dil) in enumerate(DIL_PAIRS):
            half = win // 2
            n_side = half // dil
            offs = jnp.arange(-n_side, n_side + 1) * dil
            bias = t5[t5_bucket(offs), g].T
            o, m, den = dilated_window_attention(qd[:, :, g], kd[:, :, g], vd[:, :, g], offs, bias, half)
            o_list.append(o)
            m_list.append(m)
            d_list.append(den)
        m_all = jnp.stack(m_list)
        wts = jnp.stack(d_list) * jnp.exp(m_all - jnp.max(m_all, axis=0, keepdims=True))
        out_dil = jnp.sum(wts[..., None] * jnp.stack(o_list), axis=0) / jnp.sum(wts, axis=0)[..., None]
        out_dil = out_dil.astype(x.dtype).reshape(b, s, DIL_OUT_W)

        kvm = rms_norm(mem, g_mem[l]) @ w_mem_kv[l]
        km, vm = jnp.split(kvm, 2, axis=-1)
        km = rms_norm(km.reshape(b, -1, MEM_HEADS, hd), qk_gain[l, 2, 1])
        vm = vm.reshape(b, -1, MEM_HEADS, hd)
        qm = rms_norm(qm.reshape(b, s, MEM_HEADS, hd), qk_gain[l, 2, 0])
        logits_m = jnp.einsum('bshd,bmhd->bhsm', qm, km).astype(jnp.float32) * scale
        p_m = jax.nn.softmax(logits_m, axis=-1).astype(vm.dtype)
        out_mem = jnp.einsum('bhsm,bmhd->bshd', p_m, vm).reshape(b, s, MEM_W)

        x = x + jnp.concatenate([out_na, out_dil, out_mem], axis=-1) @ w_out[l]
        x = x + hierarchical_moe(rms_norm(x, g_ffn[l]), w_r1[l], b_r1[l], w_r2[l], b_r2[l], w1[l], w3[l], w2[l])
    return x
```

```python
import functools
import math

import jax
import jax.numpy as jnp
from jax import lax
from jax.experimental import pallas as pl
from jax.experimental.pallas import tpu as pltpu

F32 = jnp.float32
BF16 = jnp.bfloat16
U32 = jnp.uint32
I32 = jnp.int32

HEAD_DIM = 64
GRID_W = 64
NA_HEADS = 8
NA_WIN_ROWS = 8
NA_WIN_COLS = 16
DIL_PAIRS = ((128, 1), (512, 4), (2048, 16))
DIL_HEADS_PER_GROUP = 4
DIL_SIDE = 64
DIL_BLOCK = 128
MEM_HEADS = 4
T5_BUCKETS = 32
T5_MAX_DIST = 1024
N_GROUPS = 4
EXPERTS_PER_GROUP = 8
N_EXPERTS = N_GROUPS * EXPERTS_PER_GROUP
MOE_BLOCK = 256
EPS = 1e-6
NEG_INF = -1e30

NA_W = NA_HEADS * HEAD_DIM
DIL_W = DIL_HEADS_PER_GROUP * HEAD_DIM
MEM_W = MEM_HEADS * HEAD_DIM
LANES = 128
CHUNK = 256
ROW_TILE = 512
VMEM_LIMIT = 48 * 1024 * 1024


def _cparams(n_axes):
    return pltpu.CompilerParams(dimension_semantics=("arbitrary",) * n_axes, vmem_limit_bytes=VMEM_LIMIT)


def _nt_dot(a, b):
    return lax.dot_general(a, b, (((1,), (1,)), ((), ())), preferred_element_type=F32)


def _pack_pair(a, b):
    ua = lax.bitcast_convert_type(a.astype(BF16).astype(F32), U32)
    ub = lax.bitcast_convert_type(b.astype(BF16).astype(F32), U32)
    return (ua >> 16) | (ub & jnp.uint32(0xFFFF0000))


def _unpack_pair(p):
    a = lax.bitcast_convert_type(p << 16, F32)
    b = lax.bitcast_convert_type(p & jnp.uint32(0xFFFF0000), F32)
    return a, b


def _proj_kernel(x_ref, g_ref, w_ref, gain_ref, bd_ref, *refs, plan, tm):
    n_buf = CHUNK // LANES
    out_refs, ybufs = refs[:-n_buf], refs[-n_buf:]
    x = x_ref[...]
    ms = jnp.mean(x * x, axis=-1, keepdims=True)
    h = (x * lax.rsqrt(ms + EPS) * g_ref[...]).astype(BF16)
    for c, (normed, oi, col, dil) in enumerate(plan):
        y = jnp.dot(h, w_ref[:, c * CHUNK:(c + 1) * CHUNK], preferred_element_type=F32)
        if normed:
            sq = y * y
            hi = sq.astype(BF16)
            lo = (sq - hi.astype(F32)).astype(BF16)
            msq = (jnp.dot(hi, bd_ref[...], preferred_element_type=F32)
                   + jnp.dot(lo, bd_ref[...], preferred_element_type=F32))
            y = y * lax.rsqrt(msq + EPS) * gain_ref[c:c + 1, :]
        o_ref = out_refs[oi]
        if dil == 1:
            o_ref[:, col:col + CHUNK] = y.astype(o_ref.dtype)
        else:
            for j, ybuf in enumerate(ybufs):
                ybuf[...] = y[:, j * LANES:(j + 1) * LANES]
            for rho in range(dil):
                for j, ybuf in enumerate(ybufs):
                    rows = ybuf[pl.ds(rho, tm // dil, stride=dil), :]
                    o_ref[0, rho, :, col + j * LANES:col + (j + 1) * LANES] = rows.astype(o_ref.dtype)


def _project(x2d, g, w_bf16, gains, plan, out_shapes, out_specs, tm):
    n, d = x2d.shape
    cols = w_bf16.shape[1]
    bd = jnp.kron(jnp.eye(CHUNK // HEAD_DIM, dtype=F32), jnp.full((HEAD_DIM, HEAD_DIM), 1.0 / HEAD_DIM, F32)).astype(BF16)
    return pl.pallas_call(
        functools.partial(_proj_kernel, plan=plan, tm=tm),
        grid=(n // tm,),
        in_specs=[
            pl.BlockSpec((tm, d), lambda i: (i, 0)),
            pl.BlockSpec((1, d), lambda i: (0, 0)),
            pl.BlockSpec((d, cols), lambda i: (0, 0)),
            pl.BlockSpec(gains.shape, lambda i: (0, 0)),
            pl.BlockSpec((CHUNK, CHUNK), lambda i: (0, 0)),
        ],
        out_specs=out_specs,
        out_shape=out_shapes,
        scratch_shapes=[pltpu.VMEM((tm, LANES), F32)] * (CHUNK // LANES),
        compiler_params=_cparams(1),
        name="rmsnorm_project",
    )(x2d, g.reshape(1, d), w_bf16, gains, bd)


def _na_kernel(q_ref, k_ref, v_ref, bias_ref, o_ref, *, rows_per_step, n_rows):
    rb = pl.program_id(2)
    keys = NA_WIN_ROWS * GRID_W

    def row_body(i, carry):
        r = rb * rows_per_step + i
        r0 = jnp.clip(r - NA_WIN_ROWS // 2, 0, n_rows - NA_WIN_ROWS)
        row_type = r - r0
        q0 = pl.multiple_of(i * GRID_W, GRID_W)
        k0 = pl.multiple_of(r0 * GRID_W, GRID_W)
        q = q_ref[0, pl.ds(q0, GRID_W), :]
        k = k_ref[0, pl.ds(k0, keys), :]
        v = v_ref[0, pl.ds(k0, keys), :]
        outs = []
        for hh in range(LANES // HEAD_DIM):
            sl = slice(hh * HEAD_DIM, (hh + 1) * HEAD_DIM)
            s = _nt_dot(q[:, sl], k[:, sl]) + bias_ref[hh, row_type]
            m = jnp.max(s, axis=-1, keepdims=True)
            e = jnp.exp(s - m)
            den = jnp.sum(e, axis=-1, keepdims=True)
            o = jnp.dot(e.astype(BF16), v[:, sl], preferred_element_type=F32)
            outs.append(o / den)
        o_ref[0, pl.ds(q0, GRID_W), :] = jnp.concatenate(outs, axis=-1).astype(o_ref.dtype)
        return carry

    lax.fori_loop(0, rows_per_step, row_body, 0)


def _na_bias_table(rpb):
    kh, kw = NA_WIN_ROWS, NA_WIN_COLS
    typ = jnp.arange(kh)[:, None, None, None]
    c = jnp.arange(GRID_W)[None, :, None, None]
    a = jnp.arange(kh)[None, None, :, None]
    kc = jnp.arange(GRID_W)[None, None, None, :]
    c0 = jnp.clip(c - kw // 2, 0, GRID_W - kw)
    inside = (kc >= c0) & (kc < c0 + kw)
    dr = jnp.broadcast_to(a - typ + (kh - 1), (kh, GRID_W, kh, GRID_W))
    dc = jnp.broadcast_to(jnp.clip(kc - c + (kw - 1), 0, 2 * kw - 2), (kh, GRID_W, kh, GRID_W))
    vals = rpb.astype(F32)[:, dr, dc]
    table = jnp.where(inside[None], vals, NEG_INF)
    return table.reshape(rpb.shape[0], kh, GRID_W, kh * GRID_W)


def _neighbourhood_attention(pm3, bias_tab, b, s):
    n_rows = s // GRID_W
    rows_per_step = 8
    tq = rows_per_step * GRID_W
    pairs = NA_W // LANES
    return pl.pallas_call(
        functools.partial(_na_kernel, rows_per_step=rows_per_step, n_rows=n_rows),
        grid=(b, pairs, n_rows // rows_per_step),
        in_specs=[
            pl.BlockSpec((1, tq, LANES), lambda bi, hp, rb: (bi, rb, hp)),
            pl.BlockSpec((1, s, LANES), lambda bi, hp, rb: (bi, 0, pairs + hp)),
            pl.BlockSpec((1, s, LANES), lambda bi, hp, rb: (bi, 0, 2 * pairs + hp)),
            pl.BlockSpec((LANES // HEAD_DIM, NA_WIN_ROWS, GRID_W, NA_WIN_ROWS * GRID_W), lambda bi, hp, rb: (hp, 0, 0, 0)),
        ],
        out_specs=pl.BlockSpec((1, tq, LANES), lambda bi, hp, rb: (bi, rb, hp)),
        out_shape=jax.ShapeDtypeStruct((b, s, NA_W), BF16),
        compiler_params=_cparams(3),
        name="neighbourhood_attention",
    )(pm3, pm3, pm3, bias_tab)


def _dil_kernel(q_ref, k_ref, v_ref, band_ref, o_ref, lse_ref, *, length):
    nblk = length // DIL_BLOCK
    span = DIL_BLOCK + 2 * DIL_SIDE

    def blk(i, carry):
        i0 = pl.multiple_of(i * DIL_BLOCK, DIL_BLOCK)
        p0 = pl.multiple_of(jnp.maximum(i0 - DIL_SIDE, 0), DIL_SIDE)
        n0 = pl.multiple_of(jnp.minimum(i0 + DIL_BLOCK, length - DIL_SIDE), DIL_SIDE)
        q = q_ref[0, 0, pl.ds(i0, DIL_BLOCK), :]
        k = jnp.concatenate([k_ref[0, 0, pl.ds(p0, DIL_SIDE), :], k_ref[0, 0, pl.ds(i0, DIL_BLOCK), :],
                             k_ref[0, 0, pl.ds(n0, DIL_SIDE), :]], axis=0)
        v = jnp.concatenate([v_ref[0, 0, pl.ds(p0, DIL_SIDE), :], v_ref[0, 0, pl.ds(i0, DIL_BLOCK), :],
                             v_ref[0, 0, pl.ds(n0, DIL_SIDE), :]], axis=0)
        kpos = i0 - DIL_SIDE + lax.broadcasted_iota(I32, (1, span), 1)
        valid = (kpos >= 0) & (kpos < length)
        outs, lses = [], []
        for hh in range(LANES // HEAD_DIM):
            sl = slice(hh * HEAD_DIM, (hh + 1) * HEAD_DIM)
            s = _nt_dot(q[:, sl], k[:, sl]) + band_ref[hh]
            s = jnp.where(valid, s, NEG_INF)
            m = jnp.max(s, axis=-1, keepdims=True)
            e = jnp.exp(s - m)
            den = jnp.sum(e, axis=-1, keepdims=True)
            o = jnp.dot(e.astype(BF16), v[:, sl], preferred_element_type=F32)
            outs.append(o / den)
            lses.append(jnp.broadcast_to(m + jnp.log(den), (DIL_BLOCK, HEAD_DIM)))
        o_ref[0, 0, pl.ds(i0, DIL_BLOCK), :] = jnp.concatenate(outs, axis=-1)
        lse_ref[0, 0, pl.ds(i0, DIL_BLOCK), :] = jnp.concatenate(lses, axis=-1)
        return carry

    lax.fori_loop(0, nblk, blk, 0)


def _t5_bucket(rel):
    nb = T5_BUCKETS // 2
    max_exact = nb // 2
    n = jnp.abs(rel)
    upper = (rel > 0).astype(I32) * nb
    nf = jnp.maximum(n, 1).astype(F32)
    large = max_exact + (jnp.log(nf / max_exact) / math.log(T5_MAX_DIST / max_exact) * (nb - max_exact)).astype(I32)
    large = jnp.minimum(large, nb - 1)
    return upper + jnp.where(n < max_exact, n, large)


def _dil_band_table(t5, g, dil):
    offs = jnp.arange(-DIL_SIDE, DIL_SIDE + 1) * dil
    bias = t5[_t5_bucket(offs), g].T.astype(F32)
    qi = jnp.arange(DIL_BLOCK)[:, None]
    kj = jnp.arange(DIL_BLOCK + 2 * DIL_SIDE)[None, :]
    j = kj - qi
    inside = (j >= 0) & (j <= 2 * DIL_SIDE)
    return jnp.where(inside[None], bias[:, jnp.clip(j, 0, 2 * DIL_SIDE)], NEG_INF)


def _dilated_attention(dg, band, b, dil, length):
    pairs = DIL_W // LANES
    spec = lambda off: pl.BlockSpec((1, 1, length, LANES), lambda bi, rho, hp: (bi, rho, 0, off + hp))
    out_spec = pl.BlockSpec((1, 1, length, LANES), lambda bi, rho, hp: (bi, rho, 0, hp))
    shape = jax.ShapeDtypeStruct((b, dil, length, DIL_W), F32)
    return pl.pallas_call(
        functools.partial(_dil_kernel, length=length),
        grid=(b, dil, pairs),
        in_specs=[spec(0), spec(pairs), spec(2 * pairs),
                  pl.BlockSpec((LANES // HEAD_DIM, DIL_BLOCK, DIL_BLOCK + 2 * DIL_SIDE), lambda bi, rho, hp: (hp, 0, 0))],
        out_specs=(out_spec, out_spec),
        out_shape=(shape, shape),
        compiler_params=_cparams(3),
        name=f"dilated_attention_d{dil}",
    )(dg, dg, dg, band)


def _mix_kernel(x_ref, na_ref, qm_ref, o0_ref, l0_ref, o1_ref, l1_ref, o2_ref, l2_ref, kvm_ref, wout_ref, gffn_ref,
                wr_ref, br_ref, x1_ref, hp_ref, route_ref, *ibufs, tm, dils):
    def token_major(ref, dil):
        if dil == 1:
            return ref[0, 0]
        for rho in range(dil):
            for j, ibuf in enumerate(ibufs):
                ibuf[pl.ds(rho, tm // dil, stride=dil), :] = ref[0, rho, :, j * LANES:(j + 1) * LANES]
        return jnp.concatenate([ibuf[...] for ibuf in ibufs], axis=-1)

    lses = []
    for l_ref, dil in zip((l0_ref, l1_ref, l2_ref), dils):
        lses.append(token_major(l_ref, dil))
    lmax = jnp.maximum(jnp.maximum(lses[0], lses[1]), lses[2])
    wts = [jnp.exp(l - lmax) for l in lses]
    num = None
    for o_ref, dil, w in zip((o0_ref, o1_ref, o2_ref), dils, wts):
        term = w * token_major(o_ref, dil)
        num = term if num is None else num + term
    out_dil = num / (wts[0] + wts[1] + wts[2])

    qm = qm_ref[...]
    mem_outs = []
    for hh in range(MEM_HEADS):
        sl = slice(hh * HEAD_DIM, (hh + 1) * HEAD_DIM)
        s = _nt_dot(qm[:, sl], kvm_ref[0, :, sl])
        m = jnp.max(s, axis=-1, keepdims=True)
        e = jnp.exp(s - m)
        den = jnp.sum(e, axis=-1, keepdims=True)
        vm = kvm_ref[0, :, MEM_W + hh * HEAD_DIM:MEM_W + (hh + 1) * HEAD_DIM]
        mem_outs.append(jnp.dot(e.astype(BF16), vm, preferred_element_type=F32) / den)
    out_mem = jnp.concatenate(mem_outs, axis=-1)

    y = jnp.dot(na_ref[...], wout_ref[0:NA_W, :], preferred_element_type=F32)
    y = y + jnp.dot(out_dil.astype(BF16), wout_ref[NA_W:NA_W + DIL_W, :], preferred_element_type=F32)
    y = y + jnp.dot(out_mem.astype(BF16), wout_ref[NA_W + DIL_W:, :], preferred_element_type=F32)
    x1 = x_ref[...] + y
    x1_ref[...] = x1

    ms = jnp.mean(x1 * x1, axis=-1, keepdims=True)
    h = x1 * lax.rsqrt(ms + EPS) * gffn_ref[...]
    half = h.shape[1] // 2
    hp_ref[...] = _pack_pair(h[:, :half], h[:, half:])
    logits = jnp.dot(h, wr_ref[...], preferred_element_type=F32, precision=lax.Precision.HIGHEST) + br_ref[...]
    lane = lax.broadcasted_iota(I32, logits.shape, 1)
    neg = -jnp.inf
    gl = jnp.where(lane < N_GROUPS, logits, neg)
    gmax = jnp.max(gl, axis=-1, keepdims=True)
    gidx = jnp.min(jnp.where(gl == gmax, lane, LANES), axis=-1, keepdims=True)
    grp_gate = 1.0 / jnp.sum(jnp.where(lane < N_GROUPS, jnp.exp(logits - gmax), 0.0), axis=-1, keepdims=True)
    lo = N_GROUPS + EXPERTS_PER_GROUP * gidx
    fl = jnp.where((lane >= lo) & (lane < lo + EXPERTS_PER_GROUP), logits, neg)
    v1 = jnp.max(fl, axis=-1, keepdims=True)
    i1 = jnp.min(jnp.where(fl == v1, lane, LANES), axis=-1, keepdims=True)
    fl2 = jnp.where(lane == i1, neg, fl)
    v2 = jnp.max(fl2, axis=-1, keepdims=True)
    i2 = jnp.min(jnp.where(fl2 == v2, lane, LANES), axis=-1, keepdims=True)
    t = jnp.exp(v2 - v1)
    g1 = grp_gate / (1.0 + t)
    g2 = grp_gate * t / (1.0 + t)
    e1 = (i1 - N_GROUPS).astype(F32)
    e2 = (i2 - N_GROUPS).astype(F32)
    route_ref[...] = jnp.where(lane == 0, e1, jnp.where(lane == 1, e2, jnp.where(lane == 2, g1, jnp.where(lane == 3, g2, 0.0))))


def _mix(x2d, na, pm, qm_col, dil_outs, kvm, w_out_bf16, g_ffn, w_r, b_r, s):
    n, d = x2d.shape
    tm = ROW_TILE
    tpb = s // tm
    dils = tuple(dil for _, dil in DIL_PAIRS)
    dil_specs, dil_args = [], []
    for (o, lse), dil in zip(dil_outs, dils):
        spec = pl.BlockSpec((1, dil, tm // dil, DIL_W), lambda i: (i // tpb, 0, i % tpb, 0))
        dil_specs += [spec, spec]
        dil_args += [o, lse]
    return pl.pallas_call(
        functools.partial(_mix_kernel, tm=tm, dils=dils),
        grid=(n // tm,),
        in_specs=[
            pl.BlockSpec((tm, d), lambda i: (i, 0)),
            pl.BlockSpec((tm, NA_W), lambda i: (i, 0)),
            pl.BlockSpec((tm, MEM_W), lambda i: (i, qm_col // MEM_W)),
            *dil_specs,
            pl.BlockSpec((1, kvm.shape[1], kvm.shape[2]), lambda i: (i // tpb, 0, 0)),
            pl.BlockSpec(w_out_bf16.shape, lambda i: (0, 0)),
            pl.BlockSpec((1, d), lambda i: (0, 0)),
            pl.BlockSpec(w_r.shape, lambda i: (0, 0)),
            pl.BlockSpec(b_r.shape, lambda i: (0, 0)),
        ],
        out_specs=(
            pl.BlockSpec((tm, d), lambda i: (i, 0)),
            pl.BlockSpec((tm, d // 2), lambda i: (i, 0)),
            pl.BlockSpec((tm, LANES), lambda i: (i, 0)),
        ),
        out_shape=(
            jax.ShapeDtypeStruct((n, d), F32),
            jax.ShapeDtypeStruct((n, d // 2), U32),
            jax.ShapeDtypeStruct((n, LANES), F32),
        ),
        scratch_shapes=[pltpu.VMEM((tm, LANES), F32)] * (DIL_W // LANES),
        compiler_params=_cparams(1),
        name="merge_memattn_outproj_router",
    )(x2d, na, pm, *dil_args, kvm, w_out_bf16, g_ffn.reshape(1, d), w_r, b_r)


def _dispatch_kernel(dest_hbm, h_ref, xs_init_hbm, xs_hbm, idx_smem, idx_sem, row_sem, *, tm):
    del xs_init_hbm
    i = pl.program_id(0)
    cp = pltpu.make_async_copy(dest_hbm.at[i], idx_smem, idx_sem)
    cp.start()
    cp.wait()

    def row_copy(t, d):
        return pltpu.make_async_copy(h_ref.at[pl.ds(t, 1)], xs_hbm.at[pl.ds(d, 1)], row_sem)

    def start(t, carry):
        row_copy(t, idx_smem[2 * t]).start()
        row_copy(t, idx_smem[2 * t + 1]).start()
        return carry

    def wait(t, carry):
        row_copy(0, 0).wait()
        row_copy(0, 0).wait()
        return carry

    lax.fori_loop(0, tm, start, 0)
    lax.fori_loop(0, tm, wait, 0)


def _dispatch(dest2d, h_packed, cap, tm):
    n, w = h_packed.shape
    xs_init = jnp.zeros((cap, w), U32)
    return pl.pallas_call(
        functools.partial(_dispatch_kernel, tm=tm),
        grid=(n // tm,),
        in_specs=[
            pl.BlockSpec(memory_space=pl.ANY),
            pl.BlockSpec((tm, w), lambda i: (i, 0)),
            pl.BlockSpec(memory_space=pl.ANY),
        ],
        out_specs=pl.BlockSpec(memory_space=pl.ANY),
        out_shape=jax.ShapeDtypeStruct((cap, w), U32),
        scratch_shapes=[pltpu.SMEM((2 * tm,), I32), pltpu.SemaphoreType.DMA, pltpu.SemaphoreType.DMA],
        input_output_aliases={2: 0},
        compiler_params=_cparams(1),
        name="moe_dispatch",
    )(dest2d, h_packed, xs_init)


def _expert_kernel(blk_exp_ref, n_used_ref, xs_ref, w1_ref, w3_ref, w2_ref, yb_ref):
    del blk_exp_ref

    @pl.when(pl.program_id(0) < n_used_ref[0])
    def _():
        a, b = _unpack_pair(xs_ref[...])
        a, b = a.astype(BF16), b.astype(BF16)
        half = a.shape[1]
        h1 = (jnp.dot(a, w1_ref[0, :half, :], preferred_element_type=F32)
              + jnp.dot(b, w1_ref[0, half:, :], preferred_element_type=F32))
        h3 = (jnp.dot(a, w3_ref[0, :half, :], preferred_element_type=F32)
              + jnp.dot(b, w3_ref[0, half:, :], preferred_element_type=F32))
        act = (h1 * jax.nn.sigmoid(h1) * h3).astype(BF16)
        y = jnp.dot(act, w2_ref[0], preferred_element_type=F32)
        yb_ref[...] = _pack_pair(y[:, :half], y[:, half:])

    @pl.when(pl.program_id(0) >= n_used_ref[0])
    def _():
        yb_ref[...] = jnp.zeros(yb_ref.shape, yb_ref.dtype)


def _experts(blk_exp, n_used, xs, w1, w3, w2):
    cap, w = xs.shape
    n_blk = cap // MOE_BLOCK
    d, de = w1.shape[1], w1.shape[2]
    row = lambda i, be, nu: (jnp.minimum(i, nu[0] - 1), 0)
    wsel = lambda i, be, nu: (be[jnp.minimum(i, nu[0] - 1)], 0, 0)
    return pl.pallas_call(
        _expert_kernel,
        grid_spec=pltpu.PrefetchScalarGridSpec(
            num_scalar_prefetch=2,
            grid=(n_blk,),
            in_specs=[
                pl.BlockSpec((MOE_BLOCK, w), row),
                pl.BlockSpec((1, d, de), wsel),
                pl.BlockSpec((1, d, de), wsel),
                pl.BlockSpec((1, de, d), wsel),
            ],
            out_specs=pl.BlockSpec((MOE_BLOCK, w), lambda i, be, nu: (i, 0)),
        ),
        out_shape=jax.ShapeDtypeStruct((cap, w), U32),
        compiler_params=_cparams(1),
        name="moe_experts",
    )(blk_exp, n_used, xs, w1, w3, w2)


def _combine_kernel(dest_hbm, yb_hbm, x1_ref, route_ref, o_ref, idx_smem, ybuf, idx_sem, row_sem, *, tm):
    i = pl.program_id(0)
    cp = pltpu.make_async_copy(dest_hbm.at[i], idx_smem, idx_sem)
    cp.start()
    cp.wait()

    def row_copy(t, k, d):
        return pltpu.make_async_copy(yb_hbm.at[pl.ds(d, 1)], ybuf.at[k, pl.ds(t, 1)], row_sem)

    def start(t, carry):
        row_copy(t, 0, idx_smem[2 * t]).start()
        row_copy(t, 1, idx_smem[2 * t + 1]).start()
        return carry

    def wait(t, carry):
        row_copy(0, 0, 0).wait()
        row_copy(0, 1, 0).wait()
        return carry

    lax.fori_loop(0, tm, start, 0)
    lax.fori_loop(0, tm, wait, 0)
    a0, b0 = _unpack_pair(ybuf[0])
    a1, b1 = _unpack_pair(ybuf[1])
    g0 = route_ref[:, 2:3]
    g1 = route_ref[:, 3:4]
    half = a0.shape[1]
    o_ref[:, :half] = x1_ref[:, :half] + (g0 * a0 + g1 * a1)
    o_ref[:, half:] = x1_ref[:, half:] + (g0 * b0 + g1 * b1)


def _combine(dest2d, yb, x1, route, tm):
    n, d = x1.shape
    w = yb.shape[1]
    return pl.pallas_call(
        functools.partial(_combine_kernel, tm=tm),
        grid=(n // tm,),
        in_specs=[
            pl.BlockSpec(memory_space=pl.ANY),
            pl.BlockSpec(memory_space=pl.ANY),
            pl.BlockSpec((tm, d), lambda i: (i, 0)),
            pl.BlockSpec((tm, LANES), lambda i: (i, 0)),
        ],
        out_specs=pl.BlockSpec((tm, d), lambda i: (i, 0)),
        out_shape=jax.ShapeDtypeStruct((n, d), F32),
        scratch_shapes=[pltpu.SMEM((2 * tm,), I32), pltpu.VMEM((2, tm, w), U32),
                        pltpu.SemaphoreType.DMA, pltpu.SemaphoreType.DMA],
        compiler_params=_cparams(1),
        name="moe_combine",
    )(dest2d, yb, x1, route)


def _routing_plan(route, n):
    e_flat = route[:, :2].astype(I32).reshape(-1)
    onehot = (e_flat[:, None] == jnp.arange(N_EXPERTS, dtype=I32)[None, :]).astype(I32)
    csum = jnp.cumsum(onehot, axis=0)
    rank = jnp.sum((csum - onehot) * onehot, axis=1)
    counts = csum[-1]
    padded = (counts + MOE_BLOCK - 1) // MOE_BLOCK * MOE_BLOCK
    pad_end = jnp.cumsum(padded)
    pad_start = pad_end - padded
    dest = pad_start[e_flat] + rank
    cap = 2 * n + N_EXPERTS * MOE_BLOCK
    n_blk = cap // MOE_BLOCK
    blk_exp = jnp.minimum(jnp.searchsorted(pad_end, jnp.arange(n_blk, dtype=I32) * MOE_BLOCK, side='right'),
                          N_EXPERTS - 1).astype(I32)
    n_used = (pad_end[-1:] // MOE_BLOCK).astype(I32)
    return dest.astype(I32), blk_exp, n_used, cap


def _layer(x, mem, g_mix, w_in, qk_gain, na_rpb, t5, g_mem, w_mem_kv, w_out, g_ffn, w_r1, b_r1, w_r2, b_r2, w1, w3, w2):
    b, s, d = x.shape
    n = b * s
    tm = ROW_TILE
    scale = HEAD_DIM ** -0.5
    dils = tuple(dil for _, dil in DIL_PAIRS)

    o_qd, o_kd, o_vd, o_qm = 3 * NA_W, 3 * NA_W + 3 * DIL_W, 3 * NA_W + 6 * DIL_W, 3 * NA_W + 9 * DIL_W
    col_idx = [jnp.arange(0, 3 * NA_W), jnp.arange(o_qm, o_qm + MEM_W)]
    for g in range(len(DIL_PAIRS)):
        for base in (o_qd, o_kd, o_vd):
            col_idx.append(jnp.arange(base + g * DIL_W, base + (g + 1) * DIL_W))
    w_perm = w_in[:, jnp.concatenate(col_idx)].astype(BF16)
    tile = lambda v: jnp.tile(v.astype(F32), CHUNK // HEAD_DIM)
    ones = jnp.ones((CHUNK,), F32)
    qa_g, ka_g = tile(qk_gain[0, 0]) * scale, tile(qk_gain[0, 1])
    qd_g, kd_g = tile(qk_gain[1, 0]) * scale, tile(qk_gain[1, 1])
    qm_g, km_g = tile(qk_gain[2, 0]) * scale, tile(qk_gain[2, 1])
    main_w = 3 * NA_W + MEM_W
    plan = [(True, 0, 0, 1), (True, 0, CHUNK, 1), (True, 0, 2 * CHUNK, 1), (True, 0, 3 * CHUNK, 1),
            (False, 0, 4 * CHUNK, 1), (False, 0, 5 * CHUNK, 1), (True, 0, 6 * CHUNK, 1)]
    gains = [qa_g, qa_g, ka_g, ka_g, ones, ones, qm_g]
    for gi, dil in enumerate(dils):
        plan += [(True, 1 + gi, 0, dil), (True, 1 + gi, DIL_W, dil), (False, 1 + gi, 2 * DIL_W, dil)]
        gains += [qd_g, kd_g, ones]
    tpb = s // tm
    out_shapes = [jax.ShapeDtypeStruct((n, main_w), BF16)]
    out_specs = [pl.BlockSpec((tm, main_w), lambda i: (i, 0))]
    for dil in dils:
        if dil == 1:
            out_shapes.append(jax.ShapeDtypeStruct((n, 3 * DIL_W), BF16))
            out_specs.append(pl.BlockSpec((tm, 3 * DIL_W), lambda i: (i, 0)))
        else:
            out_shapes.append(jax.ShapeDtypeStruct((b, dil, s // dil, 3 * DIL_W), BF16))
            out_specs.append(pl.BlockSpec((1, dil, tm // dil, 3 * DIL_W), lambda i: (i // tpb, 0, i % tpb, 0)))
    pm, *dgs = _project(x.reshape(n, d), g_mix, w_perm, jnp.stack(gains), tuple(plan), out_shapes, out_specs, tm)

    m_rows = mem.shape[0] * mem.shape[1]
    tmm = min(tm, m_rows)
    kv_plan = ((True, 0, 0, 1), (False, 0, CHUNK, 1))
    kvm = _project(mem.reshape(m_rows, d), g_mem, w_mem_kv.astype(BF16), jnp.stack([km_g, ones]), kv_plan,
                   [jax.ShapeDtypeStruct((m_rows, 2 * MEM_W), BF16)],
                   [pl.BlockSpec((tmm, 2 * MEM_W), lambda i: (i, 0))], tmm)[0]
    kvm = kvm.reshape(mem.shape[0], mem.shape[1], 2 * MEM_W)

    out_na = _neighbourhood_attention(pm.reshape(b, s, main_w), _na_bias_table(na_rpb), b, s)

    t5g = t5.reshape(T5_BUCKETS, len(DIL_PAIRS), DIL_HEADS_PER_GROUP)
    dil_outs = []
    for gi, (dg, dil) in enumerate(zip(dgs, dils)):
        dg = dg.reshape(b, dil, s // dil, 3 * DIL_W)
        dil_outs.append(_dilated_attention(dg, _dil_band_table(t5g, gi, dil), b, dil, s // dil))

    w_r = jnp.zeros((d, LANES), F32).at[:, :N_GROUPS].set(w_r1.astype(F32)).at[:, N_GROUPS:N_GROUPS + N_EXPERTS].set(w_r2.astype(F32))
    b_r = jnp.zeros((1, LANES), F32).at[0, :N_GROUPS].set(b_r1.astype(F32)).at[0, N_GROUPS:N_GROUPS + N_EXPERTS].set(b_r2.astype(F32))
    x1, h_packed, route = _mix(x.reshape(n, d), out_na.reshape(n, NA_W), pm, 3 * NA_W, dil_outs, kvm,
                               w_out.astype(BF16), g_ffn, w_r, b_r, s)

    dest, blk_exp, n_used, cap = _routing_plan(route, n)
    tmd = 256
    dest2d = dest.reshape(n // tmd, 2 * tmd)
    xs = _dispatch(dest2d, h_packed, cap, tmd)
    yb = _experts(blk_exp, n_used, xs, w1.astype(BF16), w3.astype(BF16), w2.astype(BF16))
    out = _combine(dest2d, yb, x1, route, tmd)
    return out.reshape(b, s, d)


def kernel(x, mem, g_mix, w_in, qk_gain, na_rpb, t5_table, g_mem, w_mem_kv, w_out, g_ffn, w_r1, b_r1, w_r2, b_r2, w1, w3, w2):
    for l in range(g_mix.shape[0]):
        x = _layer(x, mem, g_mix[l], w_in[l], qk_gain[l], na_rpb[l], t5_table, g_mem[l], w_mem_kv[l], w_out[l],
                   g_ffn[l], w_r1[l], b_r1[l], w_r2[l], b_r2[l], w1[l], w3[l], w2[l])
    return x
```

```python
import functools
import math

import jax
import jax.numpy as jnp
from jax import lax
from jax.experimental import pallas as pl
from jax.experimental.pallas import tpu as pltpu

F32 = jnp.float32
BF16 = jnp.bfloat16
U32 = jnp.uint32
I32 = jnp.int32

HEAD_DIM = 64
GRID_W = 64
NA_HEADS = 8
NA_WIN_ROWS = 8
NA_WIN_COLS = 16
DIL_PAIRS = ((128, 1), (512, 4), (2048, 16))
DIL_HEADS_PER_GROUP = 4
DIL_SIDE = 64
DIL_BLOCK = 128
MEM_HEADS = 4
T5_BUCKETS = 32
T5_MAX_DIST = 1024
N_GROUPS = 4
EXPERTS_PER_GROUP = 8
N_EXPERTS = N_GROUPS * EXPERTS_PER_GROUP
MOE_BLOCK = 256
EPS = 1e-6
NEG_INF = -1e30

NA_W = NA_HEADS * HEAD_DIM
DIL_W = DIL_HEADS_PER_GROUP * HEAD_DIM
MEM_W = MEM_HEADS * HEAD_DIM
LANES = 128
CHUNK = 256
ROW_TILE = 512
VMEM_LIMIT = 48 * 1024 * 1024
ROW_DMA_UNROLL = 8


def _cparams(n_axes):
    return pltpu.CompilerParams(dimension_semantics=("arbitrary",) * n_axes, vmem_limit_bytes=VMEM_LIMIT)


def _nt_dot(a, b):
    return lax.dot_general(a, b, (((1,), (1,)), ((), ())), preferred_element_type=F32)


def _pack_pair(a, b):
    ua = lax.bitcast_convert_type(a.astype(BF16).astype(F32), U32)
    ub = lax.bitcast_convert_type(b.astype(BF16).astype(F32), U32)
    return (ua >> 16) | (ub & jnp.uint32(0xFFFF0000))


def _unpack_pair(p):
    a = lax.bitcast_convert_type(p << 16, F32)
    b = lax.bitcast_convert_type(p & jnp.uint32(0xFFFF0000), F32)
    return a, b


def _proj_kernel(x_ref, g_ref, w_ref, gain_ref, bd_ref, *refs, plan, tm):
    n_buf = CHUNK // LANES
    out_refs, ybufs = refs[:-n_buf], refs[-n_buf:]
    x = x_ref[...]
    ms = jnp.mean(x * x, axis=-1, keepdims=True)
    h = (x * lax.rsqrt(ms + EPS) * g_ref[...]).astype(BF16)
    for c, (normed, oi, col, dil) in enumerate(plan):
        y = jnp.dot(h, w_ref[:, c * CHUNK:(c + 1) * CHUNK], preferred_element_type=F32)
        if normed:
            sq = y * y
            hi = sq.astype(BF16)
            lo = (sq - hi.astype(F32)).astype(BF16)
            msq = (jnp.dot(hi, bd_ref[...], preferred_element_type=F32)
                   + jnp.dot(lo, bd_ref[...], preferred_element_type=F32))
            y = y * lax.rsqrt(msq + EPS) * gain_ref[c:c + 1, :]
        o_ref = out_refs[oi]
        if dil == 1:
            o_ref[:, col:col + CHUNK] = y.astype(o_ref.dtype)
        else:
            for j, ybuf in enumerate(ybufs):
                ybuf[...] = y[:, j * LANES:(j + 1) * LANES]
            for rho in range(dil):
                for j, ybuf in enumerate(ybufs):
                    rows = ybuf[pl.ds(rho, tm // dil, stride=dil), :]
                    o_ref[0, rho, :, col + j * LANES:col + (j + 1) * LANES] = rows.astype(o_ref.dtype)


def _project(x2d, g, w_bf16, gains, plan, out_shapes, out_specs, tm):
    n, d = x2d.shape
    cols = w_bf16.shape[1]
    bd = jnp.kron(jnp.eye(CHUNK // HEAD_DIM, dtype=F32), jnp.full((HEAD_DIM, HEAD_DIM), 1.0 / HEAD_DIM, F32)).astype(BF16)
    return pl.pallas_call(
        functools.partial(_proj_kernel, plan=plan, tm=tm),
        grid=(n // tm,),
        in_specs=[
            pl.BlockSpec((tm, d), lambda i: (i, 0)),
            pl.BlockSpec((1, d), lambda i: (0, 0)),
            pl.BlockSpec((d, cols), lambda i: (0, 0)),
            pl.BlockSpec(gains.shape, lambda i: (0, 0)),
            pl.BlockSpec((CHUNK, CHUNK), lambda i: (0, 0)),
        ],
        out_specs=out_specs,
        out_shape=out_shapes,
        scratch_shapes=[pltpu.VMEM((tm, LANES), F32)] * (CHUNK // LANES),
        compiler_params=_cparams(1),
        name="rmsnorm_project",
    )(x2d, g.reshape(1, d), w_bf16, gains, bd)


def _na_kernel(q_ref, k_ref, v_ref, bias_ref, o_ref, *, rows_per_step, n_rows):
    rb = pl.program_id(2)
    keys = NA_WIN_ROWS * GRID_W

    def row_body(i, carry):
        r = rb * rows_per_step + i
        r0 = jnp.clip(r - NA_WIN_ROWS // 2, 0, n_rows - NA_WIN_ROWS)
        row_type = r - r0
        q0 = i * GRID_W
        k0 = pl.multiple_of(r0 * GRID_W, GRID_W)
        q = q_ref[0, pl.ds(q0, GRID_W), :]
        k = k_ref[0, pl.ds(k0, keys), :]
        v = v_ref[0, pl.ds(k0, keys), :]
        outs = []
        for hh in range(LANES // HEAD_DIM):
            sl = slice(hh * HEAD_DIM, (hh + 1) * HEAD_DIM)
            s = _nt_dot(q[:, sl], k[:, sl]) + bias_ref[hh, row_type]
            m = jnp.max(s, axis=-1, keepdims=True)
            e = jnp.exp(s - m)
            den = jnp.sum(e, axis=-1, keepdims=True)
            o = jnp.dot(e.astype(BF16), v[:, sl], preferred_element_type=F32)
            outs.append(o / den)
        o_ref[0, pl.ds(q0, GRID_W), :] = jnp.concatenate(outs, axis=-1).astype(o_ref.dtype)
        return carry

    for i in range(rows_per_step):
        row_body(i, 0)


def _toeplitz(g, rows, cols):
    p = g.shape[-1]
    tiled = jnp.tile(g, (1,) * (g.ndim - 1) + (rows,))[..., :rows * (p - 1)]
    return tiled.reshape(g.shape[:-1] + (rows, p - 1))[..., :cols]


def _na_bias_table(rpb):
    kh, kw = NA_WIN_ROWS, NA_WIN_COLS
    h = rpb.shape[0]
    rpb = rpb.astype(F32)
    by_row = jnp.stack([rpb[:, kh - 1 - t:2 * kh - 1 - t, :] for t in range(kh)], axis=1)
    lo = GRID_W - kw
    padded = jnp.pad(by_row, ((0, 0), (0, 0), (0, 0), (lo, 2 * GRID_W - lo - (2 * kw - 1))), constant_values=NEG_INF)
    vals = _toeplitz(jnp.roll(padded, -(GRID_W - 1), axis=-1), GRID_W, GRID_W)
    c = jnp.arange(GRID_W)[:, None]
    kc = jnp.arange(GRID_W)[None, :]
    c0 = jnp.clip(c - kw // 2, 0, GRID_W - kw)
    inside = (kc >= c0) & (kc < c0 + kw)
    table = jnp.where(inside, vals, NEG_INF)
    return table.transpose(0, 1, 3, 2, 4).reshape(h, kh, GRID_W, kh * GRID_W)


def _neighbourhood_attention(pm3, bias_tab, b, s):
    n_rows = s // GRID_W
    rows_per_step = 8
    tq = rows_per_step * GRID_W
    pairs = NA_W // LANES
    return pl.pallas_call(
        functools.partial(_na_kernel, rows_per_step=rows_per_step, n_rows=n_rows),
        grid=(b, pairs, n_rows // rows_per_step),
        in_specs=[
            pl.BlockSpec((1, tq, LANES), lambda bi, hp, rb: (bi, rb, hp)),
            pl.BlockSpec((1, s, LANES), lambda bi, hp, rb: (bi, 0, pairs + hp)),
            pl.BlockSpec((1, s, LANES), lambda bi, hp, rb: (bi, 0, 2 * pairs + hp)),
            pl.BlockSpec((LANES // HEAD_DIM, NA_WIN_ROWS, GRID_W, NA_WIN_ROWS * GRID_W), lambda bi, hp, rb: (hp, 0, 0, 0)),
        ],
        out_specs=pl.BlockSpec((1, tq, LANES), lambda bi, hp, rb: (bi, rb, hp)),
        out_shape=jax.ShapeDtypeStruct((b, s, NA_W), BF16),
        compiler_params=_cparams(3),
        name="neighbourhood_attention",
    )(pm3, pm3, pm3, bias_tab)


def _dil_kernel(q_ref, k_ref, v_ref, band_ref, o_ref, lse_ref, *, length):
    nblk = length // DIL_BLOCK
    span = DIL_BLOCK + 2 * DIL_SIDE

    def blk(i, carry):
        i0 = pl.multiple_of(i * DIL_BLOCK, DIL_BLOCK)
        p0 = pl.multiple_of(jnp.maximum(i0 - DIL_SIDE, 0), DIL_SIDE)
        n0 = pl.multiple_of(jnp.minimum(i0 + DIL_BLOCK, length - DIL_SIDE), DIL_SIDE)
        q = q_ref[0, 0, pl.ds(i0, DIL_BLOCK), :]
        k = jnp.concatenate([k_ref[0, 0, pl.ds(p0, DIL_SIDE), :], k_ref[0, 0, pl.ds(i0, DIL_BLOCK), :],
                             k_ref[0, 0, pl.ds(n0, DIL_SIDE), :]], axis=0)
        v = jnp.concatenate([v_ref[0, 0, pl.ds(p0, DIL_SIDE), :], v_ref[0, 0, pl.ds(i0, DIL_BLOCK), :],
                             v_ref[0, 0, pl.ds(n0, DIL_SIDE), :]], axis=0)
        kpos = i0 - DIL_SIDE + lax.broadcasted_iota(I32, (1, span), 1)
        valid = (kpos >= 0) & (kpos < length)
        outs, lses = [], []
        for hh in range(LANES // HEAD_DIM):
            sl = slice(hh * HEAD_DIM, (hh + 1) * HEAD_DIM)
            s = _nt_dot(q[:, sl], k[:, sl]) + band_ref[hh]
            s = jnp.where(valid, s, NEG_INF)
            m = jnp.max(s, axis=-1, keepdims=True)
            e = jnp.exp(s - m)
            den = jnp.sum(e, axis=-1, keepdims=True)
            o = jnp.dot(e.astype(BF16), v[:, sl], preferred_element_type=F32)
            outs.append(o / den)
            lses.append(jnp.broadcast_to(m + jnp.log(den), (DIL_BLOCK, HEAD_DIM)))
        o_ref[0, 0, pl.ds(i0, DIL_BLOCK), :] = jnp.concatenate(outs, axis=-1)
        lse_ref[0, 0, pl.ds(i0, DIL_BLOCK), :] = jnp.concatenate(lses, axis=-1)
        return carry

    unroll = 2 if nblk % 2 == 0 else 1

    def blk_group(j, carry):
        for u in range(unroll):
            blk(j * unroll + u, carry)
        return carry

    lax.fori_loop(0, nblk // unroll, blk_group, 0)


def _t5_bucket(rel):
    nb = T5_BUCKETS // 2
    max_exact = nb // 2
    n = jnp.abs(rel)
    upper = (rel > 0).astype(I32) * nb
    nf = jnp.maximum(n, 1).astype(F32)
    large = max_exact + (jnp.log(nf / max_exact) / math.log(T5_MAX_DIST / max_exact) * (nb - max_exact)).astype(I32)
    large = jnp.minimum(large, nb - 1)
    return upper + jnp.where(n < max_exact, n, large)


def _dil_band_table(t5, g, dil):
    offs = jnp.arange(-DIL_SIDE, DIL_SIDE + 1) * dil
    bias = t5[_t5_bucket(offs), g].T.astype(F32)
    span = DIL_BLOCK + 2 * DIL_SIDE
    period = DIL_BLOCK + span
    g_vec = jnp.pad(bias, ((0, 0), (0, period - bias.shape[1])), constant_values=NEG_INF)
    return _toeplitz(g_vec, DIL_BLOCK, span)


def _dilated_attention(dg, band, b, dil, length):
    pairs = DIL_W // LANES
    spec = lambda off: pl.BlockSpec((1, 1, length, LANES), lambda bi, rho, hp: (bi, rho, 0, off + hp))
    out_spec = pl.BlockSpec((1, 1, length, LANES), lambda bi, rho, hp: (bi, rho, 0, hp))
    shape = jax.ShapeDtypeStruct((b, dil, length, DIL_W), F32)
    return pl.pallas_call(
        functools.partial(_dil_kernel, length=length),
        grid=(b, dil, pairs),
        in_specs=[spec(0), spec(pairs), spec(2 * pairs),
                  pl.BlockSpec((LANES // HEAD_DIM, DIL_BLOCK, DIL_BLOCK + 2 * DIL_SIDE), lambda bi, rho, hp: (hp, 0, 0))],
        out_specs=(out_spec, out_spec),
        out_shape=(shape, shape),
        compiler_params=_cparams(3),
        name=f"dilated_attention_d{dil}",
    )(dg, dg, dg, band)


def _mix_kernel(x_ref, na_ref, qm_ref, o0_ref, l0_ref, o1_ref, l1_ref, o2_ref, l2_ref, kvm_ref, wout_ref, gffn_ref,
                wr_ref, br_ref, x1_ref, hp_ref, route_ref, *ibufs, tm, dils):
    def token_major(ref, dil):
        if dil == 1:
            return ref[0, 0]
        for rho in range(dil):
            for j, ibuf in enumerate(ibufs):
                ibuf[pl.ds(rho, tm // dil, stride=dil), :] = ref[0, rho, :, j * LANES:(j + 1) * LANES]
        return jnp.concatenate([ibuf[...] for ibuf in ibufs], axis=-1)

    lses = []
    for l_ref, dil in zip((l0_ref, l1_ref, l2_ref), dils):
        lses.append(token_major(l_ref, dil))
    lmax = jnp.maximum(jnp.maximum(lses[0], lses[1]), lses[2])
    wts = [jnp.exp(l - lmax) for l in lses]
    num = None
    for o_ref, dil, w in zip((o0_ref, o1_ref, o2_ref), dils, wts):
        term = w * token_major(o_ref, dil)
        num = term if num is None else num + term
    out_dil = num / (wts[0] + wts[1] + wts[2])

    qm = qm_ref[...]
    mem_outs = []
    for hh in range(MEM_HEADS):
        sl = slice(hh * HEAD_DIM, (hh + 1) * HEAD_DIM)
        s = _nt_dot(qm[:, sl], kvm_ref[0, :, sl])
        m = jnp.max(s, axis=-1, keepdims=True)
        e = jnp.exp(s - m)
        den = jnp.sum(e, axis=-1, keepdims=True)
        vm = kvm_ref[0, :, MEM_W + hh * HEAD_DIM:MEM_W + (hh + 1) * HEAD_DIM]
        mem_outs.append(jnp.dot(e.astype(BF16), vm, preferred_element_type=F32) / den)
    out_mem = jnp.concatenate(mem_outs, axis=-1)

    y = jnp.dot(na_ref[...], wout_ref[0:NA_W, :], preferred_element_type=F32)
    y = y + jnp.dot(out_dil.astype(BF16), wout_ref[NA_W:NA_W + DIL_W, :], preferred_element_type=F32)
    y = y + jnp.dot(out_mem.astype(BF16), wout_ref[NA_W + DIL_W:, :], preferred_element_type=F32)
    x1 = x_ref[...] + y
    x1_ref[...] = x1

    ms = jnp.mean(x1 * x1, axis=-1, keepdims=True)
    h = x1 * lax.rsqrt(ms + EPS) * gffn_ref[...]
    half = h.shape[1] // 2
    hp_ref[...] = _pack_pair(h[:, :half], h[:, half:])
    logits = jnp.dot(h, wr_ref[...], preferred_element_type=F32, precision=lax.Precision.HIGHEST) + br_ref[...]
    lane = lax.broadcasted_iota(I32, logits.shape, 1)
    neg = -jnp.inf
    gl = jnp.where(lane < N_GROUPS, logits, neg)
    gmax = jnp.max(gl, axis=-1, keepdims=True)
    gidx = jnp.min(jnp.where(gl == gmax, lane, LANES), axis=-1, keepdims=True)
    grp_gate = 1.0 / jnp.sum(jnp.where(lane < N_GROUPS, jnp.exp(logits - gmax), 0.0), axis=-1, keepdims=True)
    lo = N_GROUPS + EXPERTS_PER_GROUP * gidx
    fl = jnp.where((lane >= lo) & (lane < lo + EXPERTS_PER_GROUP), logits, neg)
    v1 = jnp.max(fl, axis=-1, keepdims=True)
    i1 = jnp.min(jnp.where(fl == v1, lane, LANES), axis=-1, keepdims=True)
    fl2 = jnp.where(lane == i1, neg, fl)
    v2 = jnp.max(fl2, axis=-1, keepdims=True)
    i2 = jnp.min(jnp.where(fl2 == v2, lane, LANES), axis=-1, keepdims=True)
    t = jnp.exp(v2 - v1)
    g1 = grp_gate / (1.0 + t)
    g2 = grp_gate * t / (1.0 + t)
    e1 = (i1 - N_GROUPS).astype(F32)
    e2 = (i2 - N_GROUPS).astype(F32)
    route_ref[...] = jnp.where(lane == 0, e1, jnp.where(lane == 1, e2, jnp.where(lane == 2, g1, jnp.where(lane == 3, g2, 0.0))))


def _mix(x2d, na, pm, qm_col, dil_outs, kvm, w_out_bf16, g_ffn, w_r, b_r, s):
    n, d = x2d.shape
    tm = ROW_TILE
    tpb = s // tm
    dils = tuple(dil for _, dil in DIL_PAIRS)
    dil_specs, dil_args = [], []
    for (o, lse), dil in zip(dil_outs, dils):
        spec = pl.BlockSpec((1, dil, tm // dil, DIL_W), lambda i: (i // tpb, 0, i % tpb, 0))
        dil_specs += [spec, spec]
        dil_args += [o, lse]
    return pl.pallas_call(
        functools.partial(_mix_kernel, tm=tm, dils=dils),
        grid=(n // tm,),
        in_specs=[
            pl.BlockSpec((tm, d), lambda i: (i, 0)),
            pl.BlockSpec((tm, NA_W), lambda i: (i, 0)),
            pl.BlockSpec((tm, MEM_W), lambda i: (i, qm_col // MEM_W)),
            *dil_specs,
            pl.BlockSpec((1, kvm.shape[1], kvm.shape[2]), lambda i: (i // tpb, 0, 0)),
            pl.BlockSpec(w_out_bf16.shape, lambda i: (0, 0)),
            pl.BlockSpec((1, d), lambda i: (0, 0)),
            pl.BlockSpec(w_r.shape, lambda i: (0, 0)),
            pl.BlockSpec(b_r.shape, lambda i: (0, 0)),
        ],
        out_specs=(
            pl.BlockSpec((tm, d), lambda i: (i, 0)),
            pl.BlockSpec((tm, d // 2), lambda i: (i, 0)),
            pl.BlockSpec((tm, LANES), lambda i: (i, 0)),
        ),
        out_shape=(
            jax.ShapeDtypeStruct((n, d), F32),
            jax.ShapeDtypeStruct((n, d // 2), U32),
            jax.ShapeDtypeStruct((n, LANES), F32),
        ),
        scratch_shapes=[pltpu.VMEM((tm, LANES), F32)] * (DIL_W // LANES),
        compiler_params=_cparams(1),
        name="merge_memattn_outproj_router",
    )(x2d, na, pm, *dil_args, kvm, w_out_bf16, g_ffn.reshape(1, d), w_r, b_r)


def _dispatch_kernel(dest_hbm, h_ref, xs_init_hbm, xs_hbm, idx_smem, idx_sem, row_sem, *, tm):
    del xs_init_hbm
    i = pl.program_id(0)
    cp = pltpu.make_async_copy(dest_hbm.at[i], idx_smem, idx_sem)
    cp.start()
    cp.wait()

    def row_copy(t, d):
        return pltpu.make_async_copy(h_ref.at[pl.ds(t, 1)], xs_hbm.at[pl.ds(d, 1)], row_sem)

    def start(t, carry):
        row_copy(t, idx_smem[2 * t]).start()
        row_copy(t, idx_smem[2 * t + 1]).start()
        return carry

    def wait(t, carry):
        row_copy(0, 0).wait()
        row_copy(0, 0).wait()
        return carry

    lax.fori_loop(0, tm, start, 0, unroll=ROW_DMA_UNROLL)
    lax.fori_loop(0, tm, wait, 0, unroll=ROW_DMA_UNROLL)


def _dispatch(dest2d, h_packed, cap, tm):
    n, w = h_packed.shape
    xs_init = jnp.zeros((cap, w), U32)
    return pl.pallas_call(
        functools.partial(_dispatch_kernel, tm=tm),
        grid=(n // tm,),
        in_specs=[
            pl.BlockSpec(memory_space=pl.ANY),
            pl.BlockSpec((tm, w), lambda i: (i, 0)),
            pl.BlockSpec(memory_space=pl.ANY),
        ],
        out_specs=pl.BlockSpec(memory_space=pl.ANY),
        out_shape=jax.ShapeDtypeStruct((cap, w), U32),
        scratch_shapes=[pltpu.SMEM((2 * tm,), I32), pltpu.SemaphoreType.DMA, pltpu.SemaphoreType.DMA],
        input_output_aliases={2: 0},
        compiler_params=_cparams(1),
        name="moe_dispatch",
    )(dest2d, h_packed, xs_init)


def _expert_kernel(blk_exp_ref, n_used_ref, xs_ref, w1_ref, w3_ref, w2_ref, yb_ref):
    del blk_exp_ref

    @pl.when(pl.program_id(0) < n_used_ref[0])
    def _():
        a, b = _unpack_pair(xs_ref[...])
        a, b = a.astype(BF16), b.astype(BF16)
        half = a.shape[1]
        h1 = (jnp.dot(a, w1_ref[0, :half, :], preferred_element_type=F32)
              + jnp.dot(b, w1_ref[0, half:, :], preferred_element_type=F32))
        h3 = (jnp.dot(a, w3_ref[0, :half, :], preferred_element_type=F32)
              + jnp.dot(b, w3_ref[0, half:, :], preferred_element_type=F32))
        act = (h1 * jax.nn.sigmoid(h1) * h3).astype(BF16)
        y = jnp.dot(act, w2_ref[0], preferred_element_type=F32)
        yb_ref[...] = _pack_pair(y[:, :half], y[:, half:])

    @pl.when(pl.program_id(0) >= n_used_ref[0])
    def _():
        yb_ref[...] = jnp.zeros(yb_ref.shape, yb_ref.dtype)


def _experts(blk_exp, n_used, xs, w1, w3, w2):
    cap, w = xs.shape
    n_blk = cap // MOE_BLOCK
    d, de = w1.shape[1], w1.shape[2]
    row = lambda i, be, nu: (jnp.minimum(i, nu[0] - 1), 0)
    wsel = lambda i, be, nu: (be[jnp.minimum(i, nu[0] - 1)], 0, 0)
    return pl.pallas_call(
        _expert_kernel,
        grid_spec=pltpu.PrefetchScalarGridSpec(
            num_scalar_prefetch=2,
            grid=(n_blk,),
            in_specs=[
                pl.BlockSpec((MOE_BLOCK, w), row),
                pl.BlockSpec((1, d, de), wsel),
                pl.BlockSpec((1, d, de), wsel),
                pl.BlockSpec((1, de, d), wsel),
            ],
            out_specs=pl.BlockSpec((MOE_BLOCK, w), lambda i, be, nu: (i, 0)),
        ),
        out_shape=jax.ShapeDtypeStruct((cap, w), U32),
        compiler_params=_cparams(1),
        name="moe_experts",
    )(blk_exp, n_used, xs, w1, w3, w2)


def _combine_kernel(dest_hbm, yb_hbm, x1_ref, route_ref, o_ref, idx_smem, ybuf, idx_sem, row_sem, *, tm):
    i = pl.program_id(0)
    cp = pltpu.make_async_copy(dest_hbm.at[i], idx_smem, idx_sem)
    cp.start()
    cp.wait()

    def row_copy(t, k, d):
        return pltpu.make_async_copy(yb_hbm.at[pl.ds(d, 1)], ybuf.at[k, pl.ds(t, 1)], row_sem)

    def start(t, carry):
        row_copy(t, 0, idx_smem[2 * t]).start()
        row_copy(t, 1, idx_smem[2 * t + 1]).start()
        return carry

    def wait(t, carry):
        row_copy(0, 0, 0).wait()
        row_copy(0, 1, 0).wait()
        return carry

    lax.fori_loop(0, tm, start, 0, unroll=ROW_DMA_UNROLL)
    lax.fori_loop(0, tm, wait, 0, unroll=ROW_DMA_UNROLL)
    a0, b0 = _unpack_pair(ybuf[0])
    a1, b1 = _unpack_pair(ybuf[1])
    g0 = route_ref[:, 2:3]
    g1 = route_ref[:, 3:4]
    half = a0.shape[1]
    o_ref[:, :half] = x1_ref[:, :half] + (g0 * a0 + g1 * a1)
    o_ref[:, half:] = x1_ref[:, half:] + (g0 * b0 + g1 * b1)


def _combine(dest2d, yb, x1, route, tm):
    n, d = x1.shape
    w = yb.shape[1]
    return pl.pallas_call(
        functools.partial(_combine_kernel, tm=tm),
        grid=(n // tm,),
        in_specs=[
            pl.BlockSpec(memory_space=pl.ANY),
            pl.BlockSpec(memory_space=pl.ANY),
            pl.BlockSpec((tm, d), lambda i: (i, 0)),
            pl.BlockSpec((tm, LANES), lambda i: (i, 0)),
        ],
        out_specs=pl.BlockSpec((tm, d), lambda i: (i, 0)),
        out_shape=jax.ShapeDtypeStruct((n, d), F32),
        scratch_shapes=[pltpu.SMEM((2 * tm,), I32), pltpu.VMEM((2, tm, w), U32),
                        pltpu.SemaphoreType.DMA, pltpu.SemaphoreType.DMA],
        compiler_params=_cparams(1),
        name="moe_combine",
    )(dest2d, yb, x1, route)


def _routing_plan(route, n):
    e_flat = route[:, :2].astype(I32).reshape(-1)
    blk = MOE_BLOCK
    onehot = (e_flat[:, None] == jnp.arange(N_EXPERTS, dtype=I32)[None, :]).astype(BF16).reshape(-1, blk, N_EXPERTS)
    tri = (jnp.arange(blk)[:, None] > jnp.arange(blk)[None, :]).astype(BF16)
    within = jnp.einsum('ij,bjk->bik', tri, onehot, preferred_element_type=F32)
    blk_tot = jnp.sum(onehot.astype(F32), axis=1)
    blk_base = jnp.cumsum(blk_tot, axis=0) - blk_tot
    rank = jnp.sum((within + blk_base[:, None, :]) * onehot.astype(F32), axis=-1).reshape(-1).astype(I32)
    counts = jnp.sum(blk_tot, axis=0).astype(I32)
    padded = (counts + MOE_BLOCK - 1) // MOE_BLOCK * MOE_BLOCK
    pad_end = jnp.cumsum(padded)
    pad_start = pad_end - padded
    dest = pad_start[e_flat] + rank
    cap = 2 * n + N_EXPERTS * MOE_BLOCK
    n_blk = cap // MOE_BLOCK
    blk_exp = jnp.minimum(jnp.searchsorted(pad_end, jnp.arange(n_blk, dtype=I32) * MOE_BLOCK, side='right'),
                          N_EXPERTS - 1).astype(I32)
    n_used = (pad_end[-1:] // MOE_BLOCK).astype(I32)
    return dest.astype(I32), blk_exp, n_used, cap


def _layer(x, mem, g_mix, w_in, qk_gain, na_rpb, t5, g_mem, w_mem_kv, w_out, g_ffn, w_r1, b_r1, w_r2, b_r2, w1, w3, w2):
    b, s, d = x.shape
    n = b * s
    tm = ROW_TILE
    scale = HEAD_DIM ** -0.5
    dils = tuple(dil for _, dil in DIL_PAIRS)

    o_qd, o_kd, o_vd, o_qm = 3 * NA_W, 3 * NA_W + 3 * DIL_W, 3 * NA_W + 6 * DIL_W, 3 * NA_W + 9 * DIL_W
    w_bf = w_in.astype(BF16)
    col_blocks = [w_bf[:, :3 * NA_W], w_bf[:, o_qm:o_qm + MEM_W]]
    for g in range(len(DIL_PAIRS)):
        for base in (o_qd, o_kd, o_vd):
            col_blocks.append(w_bf[:, base + g * DIL_W:base + (g + 1) * DIL_W])
    w_perm = jnp.concatenate(col_blocks, axis=1)
    tile = lambda v: jnp.tile(v.astype(F32), CHUNK // HEAD_DIM)
    ones = jnp.ones((CHUNK,), F32)
    qa_g, ka_g = tile(qk_gain[0, 0]) * scale, tile(qk_gain[0, 1])
    qd_g, kd_g = tile(qk_gain[1, 0]) * scale, tile(qk_gain[1, 1])
    qm_g, km_g = tile(qk_gain[2, 0]) * scale, tile(qk_gain[2, 1])
    main_w = 3 * NA_W + MEM_W
    plan = [(True, 0, 0, 1), (True, 0, CHUNK, 1), (True, 0, 2 * CHUNK, 1), (True, 0, 3 * CHUNK, 1),
            (False, 0, 4 * CHUNK, 1), (False, 0, 5 * CHUNK, 1), (True, 0, 6 * CHUNK, 1)]
    gains = [qa_g, qa_g, ka_g, ka_g, ones, ones, qm_g]
    for gi, dil in enumerate(dils):
        plan += [(True, 1 + gi, 0, dil), (True, 1 + gi, DIL_W, dil), (False, 1 + gi, 2 * DIL_W, dil)]
        gains += [qd_g, kd_g, ones]
    tpb = s // tm
    out_shapes = [jax.ShapeDtypeStruct((n, main_w), BF16)]
    out_specs = [pl.BlockSpec((tm, main_w), lambda i: (i, 0))]
    for dil in dils:
        if dil == 1:
            out_shapes.append(jax.ShapeDtypeStruct((n, 3 * DIL_W), BF16))
            out_specs.append(pl.BlockSpec((tm, 3 * DIL_W), lambda i: (i, 0)))
        else:
            out_shapes.append(jax.ShapeDtypeStruct((b, dil, s // dil, 3 * DIL_W), BF16))
            out_specs.append(pl.BlockSpec((1, dil, tm // dil, 3 * DIL_W), lambda i: (i // tpb, 0, i % tpb, 0)))
    pm, *dgs = _project(x.reshape(n, d), g_mix, w_perm, jnp.stack(gains), tuple(plan), out_shapes, out_specs, tm)

    m_rows = mem.shape[0] * mem.shape[1]
    tmm = min(tm, m_rows)
    kv_plan = ((True, 0, 0, 1), (False, 0, CHUNK, 1))
    kvm = _project(mem.reshape(m_rows, d), g_mem, w_mem_kv.astype(BF16), jnp.stack([km_g, ones]), kv_plan,
                   [jax.ShapeDtypeStruct((m_rows, 2 * MEM_W), BF16)],
                   [pl.BlockSpec((tmm, 2 * MEM_W), lambda i: (i, 0))], tmm)[0]
    kvm = kvm.reshape(mem.shape[0], mem.shape[1], 2 * MEM_W)

    out_na = _neighbourhood_attention(pm.reshape(b, s, main_w), _na_bias_table(na_rpb), b, s)

    t5g = t5.reshape(T5_BUCKETS, len(DIL_PAIRS), DIL_HEADS_PER_GROUP)
    dil_outs = []
    for gi, (dg, dil) in enumerate(zip(dgs, dils)):
        dg = dg.reshape(b, dil, s // dil, 3 * DIL_W)
        dil_outs.append(_dilated_attention(dg, _dil_band_table(t5g, gi, dil), b, dil, s // dil))

    w_r = jnp.zeros((d, LANES), F32).at[:, :N_GROUPS].set(w_r1.astype(F32)).at[:, N_GROUPS:N_GROUPS + N_EXPERTS].set(w_r2.astype(F32))
    b_r = jnp.zeros((1, LANES), F32).at[0, :N_GROUPS].set(b_r1.astype(F32)).at[0, N_GROUPS:N_GROUPS + N_EXPERTS].set(b_r2.astype(F32))
    x1, h_packed, route = _mix(x.reshape(n, d), out_na.reshape(n, NA_W), pm, 3 * NA_W, dil_outs, kvm,
                               w_out.astype(BF16), g_ffn, w_r, b_r, s)

    dest, blk_exp, n_used, cap = _routing_plan(route, n)
    tmd = 256
    dest2d = dest.reshape(n // tmd, 2 * tmd)
    xs = _dispatch(dest2d, h_packed, cap, tmd)
    yb = _experts(blk_exp, n_used, xs, w1.astype(BF16), w3.astype(BF16), w2.astype(BF16))
    out = _combine(dest2d, yb, x1, route, tmd)
    return out.reshape(b, s, d)


def kernel(x, mem, g_mix, w_in, qk_gain, na_rpb, t5_table, g_mem, w_mem_kv, w_out, g_ffn, w_r1, b_r1, w_r2, b_r2, w1, w3, w2):
    for l in range(g_mix.shape[0]):
        x = _layer(x, mem, g_mix[l], w_in[l], qk_gain[l], na_rpb[l], t5_table, g_mem[l], w_mem_kv[l], w_out[l],
                   g_ffn[l], w_r1[l], b_r1[l], w_r2[l], b_r2[l], w1[l], w3[l], w2[l])
    return x
```

```python
import functools
import math

import jax
import jax.numpy as jnp
from jax import lax
from jax.experimental import pallas as pl
from jax.experimental.pallas import tpu as pltpu

F32 = jnp.float32
BF16 = jnp.bfloat16
U32 = jnp.uint32
I32 = jnp.int32

HEAD_DIM = 64
GRID_W = 64
NA_HEADS = 8
NA_WIN_ROWS = 8
NA_WIN_COLS = 16
NA_GROUP_ROWS = 4
NA_KEY_ROWS = NA_WIN_ROWS + NA_GROUP_ROWS - 1
DIL_PAIRS = ((128, 1), (512, 4), (2048, 16))
DIL_HEADS_PER_GROUP = 4
DIL_SIDE = 64
DIL_BLOCK = 128
MEM_HEADS = 4
T5_BUCKETS = 32
T5_MAX_DIST = 1024
N_GROUPS = 4
EXPERTS_PER_GROUP = 8
N_EXPERTS = N_GROUPS * EXPERTS_PER_GROUP
MOE_BLOCK = 256
EPS = 1e-6
NEG_INF = -1e30

NA_W = NA_HEADS * HEAD_DIM
DIL_W = DIL_HEADS_PER_GROUP * HEAD_DIM
MEM_W = MEM_HEADS * HEAD_DIM
LANES = 128
CHUNK = 256
ROW_TILE = 512
VMEM_LIMIT = 48 * 1024 * 1024
ROW_DMA_UNROLL = 8


def _cparams(n_axes):
    return pltpu.CompilerParams(dimension_semantics=("arbitrary",) * n_axes, vmem_limit_bytes=VMEM_LIMIT)


def _nt_dot(a, b):
    return lax.dot_general(a, b, (((1,), (1,)), ((), ())), preferred_element_type=F32)


def _pack_pair(a, b):
    ua = lax.bitcast_convert_type(a.astype(BF16).astype(F32), U32)
    ub = lax.bitcast_convert_type(b.astype(BF16).astype(F32), U32)
    return (ua >> 16) | (ub & jnp.uint32(0xFFFF0000))


def _unpack_pair(p):
    a = lax.bitcast_convert_type(p << 16, F32)
    b = lax.bitcast_convert_type(p & jnp.uint32(0xFFFF0000), F32)
    return a, b


def _proj_kernel(x_ref, g_ref, w_ref, gain_ref, bd_ref, *refs, plan, tm):
    n_buf = CHUNK // LANES
    out_refs, ybufs = refs[:-n_buf], refs[-n_buf:]
    x = x_ref[...]
    ms = jnp.mean(x * x, axis=-1, keepdims=True)
    h = (x * lax.rsqrt(ms + EPS) * g_ref[...]).astype(BF16)
    for c, (normed, oi, col, dil) in enumerate(plan):
        y = jnp.dot(h, w_ref[:, c * CHUNK:(c + 1) * CHUNK], preferred_element_type=F32)
        if normed:
            msq = jnp.dot((y * y).astype(BF16), bd_ref[...], preferred_element_type=F32)
            y = y * lax.rsqrt(msq + EPS) * gain_ref[c:c + 1, :]
        o_ref = out_refs[oi]
        if dil == 1:
            o_ref[:, col:col + CHUNK] = y.astype(o_ref.dtype)
        else:
            for j, ybuf in enumerate(ybufs):
                ybuf[...] = y[:, j * LANES:(j + 1) * LANES]
            for rho in range(dil):
                for j, ybuf in enumerate(ybufs):
                    rows = ybuf[pl.ds(rho, tm // dil, stride=dil), :]
                    o_ref[0, rho, :, col + j * LANES:col + (j + 1) * LANES] = rows.astype(o_ref.dtype)


def _project(x2d, g, w_bf16, gains, plan, out_shapes, out_specs, tm):
    n, d = x2d.shape
    cols = w_bf16.shape[1]
    bd = jnp.kron(jnp.eye(CHUNK // HEAD_DIM, dtype=F32), jnp.full((HEAD_DIM, HEAD_DIM), 1.0 / HEAD_DIM, F32)).astype(BF16)
    return pl.pallas_call(
        functools.partial(_proj_kernel, plan=plan, tm=tm),
        grid=(n // tm,),
        in_specs=[
            pl.BlockSpec((tm, d), lambda i: (i, 0)),
            pl.BlockSpec((1, d), lambda i: (0, 0)),
            pl.BlockSpec((d, cols), lambda i: (0, 0)),
            pl.BlockSpec(gains.shape, lambda i: (0, 0)),
            pl.BlockSpec((CHUNK, CHUNK), lambda i: (0, 0)),
        ],
        out_specs=out_specs,
        out_shape=out_shapes,
        scratch_shapes=[pltpu.VMEM((tm, LANES), F32)] * (CHUNK // LANES),
        compiler_params=_cparams(1),
        name="rmsnorm_project",
    )(x2d, g.reshape(1, d), w_bf16, gains, bd)


def _na_kernel(q_ref, k_ref, v_ref, bias_ref, o_ref, *, groups_per_step, n_rows):
    step = pl.program_id(2)
    n_groups = n_rows // NA_GROUP_ROWS
    tq = NA_GROUP_ROWS * GRID_W
    keys = NA_KEY_ROWS * GRID_W
    for i in range(groups_per_step):
        rg = step * groups_per_step + i
        key_row0 = jnp.clip(rg * NA_GROUP_ROWS - NA_WIN_ROWS // 2, 0, n_rows - NA_KEY_ROWS)
        gtype = jnp.where(rg == 0, 0, jnp.where(rg == n_groups - 1, 2, 1))
        k0 = pl.multiple_of(key_row0 * GRID_W, GRID_W)
        q = q_ref[0, i * tq:(i + 1) * tq, :]
        k = k_ref[0, pl.ds(k0, keys), :]
        v = v_ref[0, pl.ds(k0, keys), :]
        outs = []
        for hh in range(LANES // HEAD_DIM):
            sl = slice(hh * HEAD_DIM, (hh + 1) * HEAD_DIM)
            s = _nt_dot(q[:, sl], k[:, sl]) + bias_ref[hh, gtype]
            m = jnp.max(s, axis=-1, keepdims=True)
            e = jnp.exp(s - m)
            den = jnp.sum(e, axis=-1, keepdims=True)
            o = jnp.dot(e.astype(BF16), v[:, sl], preferred_element_type=F32)
            outs.append(o / den)
        o_ref[0, i * tq:(i + 1) * tq, :] = jnp.concatenate(outs, axis=-1).astype(o_ref.dtype)


def _toeplitz(g, rows, cols):
    p = g.shape[-1]
    tiled = jnp.tile(g, (1,) * (g.ndim - 1) + (rows,))[..., :rows * (p - 1)]
    return tiled.reshape(g.shape[:-1] + (rows, p - 1))[..., :cols]


def _na_bias_table(rpb):
    kh, kw = NA_WIN_ROWS, NA_WIN_COLS
    h = rpb.shape[0]
    rpb = rpb.astype(F32)
    by_row = jnp.stack([rpb[:, kh - 1 - t:2 * kh - 1 - t, :] for t in range(kh)], axis=1)
    lo = GRID_W - kw
    padded = jnp.pad(by_row, ((0, 0), (0, 0), (0, 0), (lo, 2 * GRID_W - lo - (2 * kw - 1))), constant_values=NEG_INF)
    vals = _toeplitz(jnp.roll(padded, -(GRID_W - 1), axis=-1), GRID_W, GRID_W)
    c = jnp.arange(GRID_W)[:, None]
    kc = jnp.arange(GRID_W)[None, :]
    c0 = jnp.clip(c - kw // 2, 0, GRID_W - kw)
    inside = (kc >= c0) & (kc < c0 + kw)
    per_row = jnp.where(inside, vals, NEG_INF).transpose(0, 1, 3, 2, 4)

    def placed(t, shift):
        return jnp.pad(per_row[:, t], ((0, 0), (0, 0), (shift, NA_KEY_ROWS - kh - shift), (0, 0)), constant_values=NEG_INF)

    mid = kh // 2
    top = jnp.stack([placed(i, 0) for i in range(NA_GROUP_ROWS)], axis=1)
    interior = jnp.stack([placed(mid, i) for i in range(NA_GROUP_ROWS)], axis=1)
    bottom = jnp.stack([placed(mid + i, NA_KEY_ROWS - kh) for i in range(NA_GROUP_ROWS)], axis=1)
    table = jnp.stack([top, interior, bottom], axis=1)
    return table.reshape(h, 3, NA_GROUP_ROWS * GRID_W, NA_KEY_ROWS * GRID_W)


def _neighbourhood_attention(pm3, bias_tab, b, s):
    n_rows = s // GRID_W
    groups_per_step = 2
    tq = groups_per_step * NA_GROUP_ROWS * GRID_W
    pairs = NA_W // LANES
    assert n_rows % (groups_per_step * NA_GROUP_ROWS) == 0 and n_rows >= NA_KEY_ROWS + NA_GROUP_ROWS
    return pl.pallas_call(
        functools.partial(_na_kernel, groups_per_step=groups_per_step, n_rows=n_rows),
        grid=(b, pairs, n_rows // (groups_per_step * NA_GROUP_ROWS)),
        in_specs=[
            pl.BlockSpec((1, tq, LANES), lambda bi, hp, rb: (bi, rb, hp)),
            pl.BlockSpec((1, s, LANES), lambda bi, hp, rb: (bi, 0, pairs + hp)),
            pl.BlockSpec((1, s, LANES), lambda bi, hp, rb: (bi, 0, 2 * pairs + hp)),
            pl.BlockSpec((LANES // HEAD_DIM, 3, NA_GROUP_ROWS * GRID_W, NA_KEY_ROWS * GRID_W), lambda bi, hp, rb: (hp, 0, 0, 0)),
        ],
        out_specs=pl.BlockSpec((1, tq, LANES), lambda bi, hp, rb: (bi, rb, hp)),
        out_shape=jax.ShapeDtypeStruct((b, s, NA_W), BF16),
        compiler_params=_cparams(3),
        name="neighbourhood_attention",
    )(pm3, pm3, pm3, bias_tab)


def _dil_kernel(q_ref, k_ref, v_ref, band_ref, o_ref, lse_ref, *, length):
    nblk = length // DIL_BLOCK
    span = DIL_BLOCK + 2 * DIL_SIDE

    def blk(i, carry):
        i0 = pl.multiple_of(i * DIL_BLOCK, DIL_BLOCK)
        p0 = pl.multiple_of(jnp.maximum(i0 - DIL_SIDE, 0), DIL_SIDE)
        n0 = pl.multiple_of(jnp.minimum(i0 + DIL_BLOCK, length - DIL_SIDE), DIL_SIDE)
        q = q_ref[0, 0, pl.ds(i0, DIL_BLOCK), :]
        k = jnp.concatenate([k_ref[0, 0, pl.ds(p0, DIL_SIDE), :], k_ref[0, 0, pl.ds(i0, DIL_BLOCK), :],
                             k_ref[0, 0, pl.ds(n0, DIL_SIDE), :]], axis=0)
        v = jnp.concatenate([v_ref[0, 0, pl.ds(p0, DIL_SIDE), :], v_ref[0, 0, pl.ds(i0, DIL_BLOCK), :],
                             v_ref[0, 0, pl.ds(n0, DIL_SIDE), :]], axis=0)
        kpos = i0 - DIL_SIDE + lax.broadcasted_iota(I32, (1, span), 1)
        valid = (kpos >= 0) & (kpos < length)
        outs, lses = [], []
        for hh in range(LANES // HEAD_DIM):
            sl = slice(hh * HEAD_DIM, (hh + 1) * HEAD_DIM)
            s = _nt_dot(q[:, sl], k[:, sl]) + band_ref[hh]
            s = jnp.where(valid, s, NEG_INF)
            m = jnp.max(s, axis=-1, keepdims=True)
            e = jnp.exp(s - m)
            den = jnp.sum(e, axis=-1, keepdims=True)
            o = jnp.dot(e.astype(BF16), v[:, sl], preferred_element_type=F32)
            outs.append(o / den)
            lses.append(jnp.broadcast_to(m + jnp.log(den), (DIL_BLOCK, HEAD_DIM)))
        o_ref[0, 0, pl.ds(i0, DIL_BLOCK), :] = jnp.concatenate(outs, axis=-1)
        lse_ref[0, 0, pl.ds(i0, DIL_BLOCK), :] = jnp.concatenate(lses, axis=-1)
        return carry

    unroll = 2 if nblk % 2 == 0 else 1

    def blk_group(j, carry):
        for u in range(unroll):
            blk(j * unroll + u, carry)
        return carry

    lax.fori_loop(0, nblk // unroll, blk_group, 0)


def _t5_bucket(rel):
    nb = T5_BUCKETS // 2
    max_exact = nb // 2
    n = jnp.abs(rel)
    upper = (rel > 0).astype(I32) * nb
    nf = jnp.maximum(n, 1).astype(F32)
    large = max_exact + (jnp.log(nf / max_exact) / math.log(T5_MAX_DIST / max_exact) * (nb - max_exact)).astype(I32)
    large = jnp.minimum(large, nb - 1)
    return upper + jnp.where(n < max_exact, n, large)


def _dil_band_table(t5, g, dil):
    offs = jnp.arange(-DIL_SIDE, DIL_SIDE + 1) * dil
    bias = t5[_t5_bucket(offs), g].T.astype(F32)
    span = DIL_BLOCK + 2 * DIL_SIDE
    period = DIL_BLOCK + span
    g_vec = jnp.pad(bias, ((0, 0), (0, period - bias.shape[1])), constant_values=NEG_INF)
    return _toeplitz(g_vec, DIL_BLOCK, span)


def _dilated_attention(dg, band, b, dil, length):
    pairs = DIL_W // LANES
    spec = lambda off: pl.BlockSpec((1, 1, length, LANES), lambda bi, rho, hp: (bi, rho, 0, off + hp))
    out_spec = pl.BlockSpec((1, 1, length, LANES), lambda bi, rho, hp: (bi, rho, 0, hp))
    shape = jax.ShapeDtypeStruct((b, dil, length, DIL_W), F32)
    return pl.pallas_call(
        functools.partial(_dil_kernel, length=length),
        grid=(b, dil, pairs),
        in_specs=[spec(0), spec(pairs), spec(2 * pairs),
                  pl.BlockSpec((LANES // HEAD_DIM, DIL_BLOCK, DIL_BLOCK + 2 * DIL_SIDE), lambda bi, rho, hp: (hp, 0, 0))],
        out_specs=(out_spec, out_spec),
        out_shape=(shape, shape),
        compiler_params=_cparams(3),
        name=f"dilated_attention_d{dil}",
    )(dg, dg, dg, band)


def _mix_kernel(x_ref, na_ref, qm_ref, o0_ref, l0_ref, o1_ref, l1_ref, o2_ref, l2_ref, kvm_ref, wout_ref, gffn_ref,
                wr_ref, br_ref, x1_ref, hp_ref, route_ref, *ibufs, tm, dils):
    def token_major(ref, dil):
        if dil == 1:
            return ref[0, 0]
        for rho in range(dil):
            for j, ibuf in enumerate(ibufs):
                ibuf[pl.ds(rho, tm // dil, stride=dil), :] = ref[0, rho, :, j * LANES:(j + 1) * LANES]
        return jnp.concatenate([ibuf[...] for ibuf in ibufs], axis=-1)

    lses = []
    for l_ref, dil in zip((l0_ref, l1_ref, l2_ref), dils):
        lses.append(token_major(l_ref, dil))
    lmax = jnp.maximum(jnp.maximum(lses[0], lses[1]), lses[2])
    wts = [jnp.exp(l - lmax) for l in lses]
    num = None
    for o_ref, dil, w in zip((o0_ref, o1_ref, o2_ref), dils, wts):
        term = w * token_major(o_ref, dil)
        num = term if num is None else num + term
    out_dil = num / (wts[0] + wts[1] + wts[2])

    qm = qm_ref[...]
    mem_outs = []
    for hh in range(MEM_HEADS):
        sl = slice(hh * HEAD_DIM, (hh + 1) * HEAD_DIM)
        s = _nt_dot(qm[:, sl], kvm_ref[0, :, sl])
        m = jnp.max(s, axis=-1, keepdims=True)
        e = jnp.exp(s - m)
        den = jnp.sum(e, axis=-1, keepdims=True)
        vm = kvm_ref[0, :, MEM_W + hh * HEAD_DIM:MEM_W + (hh + 1) * HEAD_DIM]
        mem_outs.append(jnp.dot(e.astype(BF16), vm, preferred_element_type=F32) / den)
    out_mem = jnp.concatenate(mem_outs, axis=-1)

    y = jnp.dot(na_ref[...], wout_ref[0:NA_W, :], preferred_element_type=F32)
    y = y + jnp.dot(out_dil.astype(BF16), wout_ref[NA_W:NA_W + DIL_W, :], preferred_element_type=F32)
    y = y + jnp.dot(out_mem.astype(BF16), wout_ref[NA_W + DIL_W:, :], preferred_element_type=F32)
    x1 = x_ref[...] + y
    x1_ref[...] = x1

    ms = jnp.mean(x1 * x1, axis=-1, keepdims=True)
    h = x1 * lax.rsqrt(ms + EPS) * gffn_ref[...]
    half = h.shape[1] // 2
    hp_ref[...] = _pack_pair(h[:, :half], h[:, half:])
    logits = jnp.dot(h, wr_ref[...], preferred_element_type=F32, precision=lax.Precision.HIGHEST) + br_ref[...]
    lane = lax.broadcasted_iota(I32, logits.shape, 1)
    neg = -jnp.inf
    gl = jnp.where(lane < N_GROUPS, logits, neg)
    gmax = jnp.max(gl, axis=-1, keepdims=True)
    gidx = jnp.min(jnp.where(gl == gmax, lane, LANES), axis=-1, keepdims=True)
    grp_gate = 1.0 / jnp.sum(jnp.where(lane < N_GROUPS, jnp.exp(logits - gmax), 0.0), axis=-1, keepdims=True)
    lo = N_GROUPS + EXPERTS_PER_GROUP * gidx
    fl = jnp.where((lane >= lo) & (lane < lo + EXPERTS_PER_GROUP), logits, neg)
    v1 = jnp.max(fl, axis=-1, keepdims=True)
    i1 = jnp.min(jnp.where(fl == v1, lane, LANES), axis=-1, keepdims=True)
    fl2 = jnp.where(lane == i1, neg, fl)
    v2 = jnp.max(fl2, axis=-1, keepdims=True)
    i2 = jnp.min(jnp.where(fl2 == v2, lane, LANES), axis=-1, keepdims=True)
    t = jnp.exp(v2 - v1)
    g1 = grp_gate / (1.0 + t)
    g2 = grp_gate * t / (1.0 + t)
    e1 = (i1 - N_GROUPS).astype(F32)
    e2 = (i2 - N_GROUPS).astype(F32)
    route_ref[...] = jnp.where(lane == 0, e1, jnp.where(lane == 1, e2, jnp.where(lane == 2, g1, jnp.where(lane == 3, g2, 0.0))))


def _mix(x2d, na, pm, qm_col, dil_outs, kvm, w_out_bf16, g_ffn, w_r, b_r, s):
    n, d = x2d.shape
    tm = ROW_TILE
    tpb = s // tm
    dils = tuple(dil for _, dil in DIL_PAIRS)
    dil_specs, dil_args = [], []
    for (o, lse), dil in zip(dil_outs, dils):
        spec = pl.BlockSpec((1, dil, tm // dil, DIL_W), lambda i: (i // tpb, 0, i % tpb, 0))
        dil_specs += [spec, spec]
        dil_args += [o, lse]
    return pl.pallas_call(
        functools.partial(_mix_kernel, tm=tm, dils=dils),
        grid=(n // tm,),
        in_specs=[
            pl.BlockSpec((tm, d), lambda i: (i, 0)),
            pl.BlockSpec((tm, NA_W), lambda i: (i, 0)),
            pl.BlockSpec((tm, MEM_W), lambda i: (i, qm_col // MEM_W)),
            *dil_specs,
            pl.BlockSpec((1, kvm.shape[1], kvm.shape[2]), lambda i: (i // tpb, 0, 0)),
            pl.BlockSpec(w_out_bf16.shape, lambda i: (0, 0)),
            pl.BlockSpec((1, d), lambda i: (0, 0)),
            pl.BlockSpec(w_r.shape, lambda i: (0, 0)),
            pl.BlockSpec(b_r.shape, lambda i: (0, 0)),
        ],
        out_specs=(
            pl.BlockSpec((tm, d), lambda i: (i, 0)),
            pl.BlockSpec((tm, d // 2), lambda i: (i, 0)),
            pl.BlockSpec((tm, LANES), lambda i: (i, 0)),
        ),
        out_shape=(
            jax.ShapeDtypeStruct((n, d), F32),
            jax.ShapeDtypeStruct((n, d // 2), U32),
            jax.ShapeDtypeStruct((n, LANES), F32),
        ),
        scratch_shapes=[pltpu.VMEM((tm, LANES), F32)] * (DIL_W // LANES),
        compiler_params=_cparams(1),
        name="merge_memattn_outproj_router",
    )(x2d, na, pm, *dil_args, kvm, w_out_bf16, g_ffn.reshape(1, d), w_r, b_r)


def _dispatch_kernel(dest_hbm, h_ref, xs_init_hbm, xs_hbm, idx_smem, idx_sem, row_sem, *, tm):
    del xs_init_hbm
    i = pl.program_id(0)
    cp = pltpu.make_async_copy(dest_hbm.at[i], idx_smem, idx_sem)
    cp.start()
    cp.wait()

    def row_copy(t, d):
        return pltpu.make_async_copy(h_ref.at[pl.ds(t, 1)], xs_hbm.at[pl.ds(d, 1)], row_sem)

    def start(t, carry):
        row_copy(t, idx_smem[2 * t]).start(priority=0)
        row_copy(t, idx_smem[2 * t + 1]).start(priority=1)
        return carry

    def wait(t, carry):
        row_copy(0, 0).wait()
        row_copy(0, 0).wait()
        return carry

    lax.fori_loop(0, tm, start, 0, unroll=ROW_DMA_UNROLL)
    lax.fori_loop(0, tm, wait, 0, unroll=ROW_DMA_UNROLL)


def _dispatch(dest2d, h_packed, cap, tm):
    n, w = h_packed.shape
    xs_init = jnp.zeros((cap, w), U32)
    return pl.pallas_call(
        functools.partial(_dispatch_kernel, tm=tm),
        grid=(n // tm,),
        in_specs=[
            pl.BlockSpec(memory_space=pl.ANY),
            pl.BlockSpec((tm, w), lambda i: (i, 0)),
            pl.BlockSpec(memory_space=pl.ANY),
        ],
        out_specs=pl.BlockSpec(memory_space=pl.ANY),
        out_shape=jax.ShapeDtypeStruct((cap, w), U32),
        scratch_shapes=[pltpu.SMEM((2 * tm,), I32), pltpu.SemaphoreType.DMA, pltpu.SemaphoreType.DMA],
        input_output_aliases={2: 0},
        compiler_params=_cparams(1),
        name="moe_dispatch",
    )(dest2d, h_packed, xs_init)


def _expert_kernel(blk_exp_ref, n_used_ref, xs_ref, w1_ref, w3_ref, w2_ref, yb_ref):
    del blk_exp_ref

    @pl.when(pl.program_id(0) < n_used_ref[0])
    def _():
        a, b = _unpack_pair(xs_ref[...])
        a, b = a.astype(BF16), b.astype(BF16)
        half = a.shape[1]
        h1 = (jnp.dot(a, w1_ref[0, :half, :], preferred_element_type=F32)
              + jnp.dot(b, w1_ref[0, half:, :], preferred_element_type=F32))
        h3 = (jnp.dot(a, w3_ref[0, :half, :], preferred_element_type=F32)
              + jnp.dot(b, w3_ref[0, half:, :], preferred_element_type=F32))
        act = (h1 * jax.nn.sigmoid(h1) * h3).astype(BF16)
        y = jnp.dot(act, w2_ref[0], preferred_element_type=F32)
        yb_ref[...] = _pack_pair(y[:, :half], y[:, half:])

    @pl.when(pl.program_id(0) >= n_used_ref[0])
    def _():
        yb_ref[...] = jnp.zeros(yb_ref.shape, yb_ref.dtype)


def _experts(blk_exp, n_used, xs, w1, w3, w2):
    cap, w = xs.shape
    n_blk = cap // MOE_BLOCK
    d, de = w1.shape[1], w1.shape[2]
    row = lambda i, be, nu: (jnp.minimum(i, nu[0] - 1), 0)
    wsel = lambda i, be, nu: (be[jnp.minimum(i, nu[0] - 1)], 0, 0)
    return pl.pallas_call(
        _expert_kernel,
        grid_spec=pltpu.PrefetchScalarGridSpec(
            num_scalar_prefetch=2,
            grid=(n_blk,),
            in_specs=[
                pl.BlockSpec((MOE_BLOCK, w), row),
                pl.BlockSpec((1, d, de), wsel),
                pl.BlockSpec((1, d, de), wsel),
                pl.BlockSpec((1, de, d), wsel),
            ],
            out_specs=pl.BlockSpec((MOE_BLOCK, w), lambda i, be, nu: (i, 0)),
        ),
        out_shape=jax.ShapeDtypeStruct((cap, w), U32),
        compiler_params=_cparams(1),
        name="moe_experts",
    )(blk_exp, n_used, xs, w1, w3, w2)


def _combine_kernel(dest_hbm, yb_hbm, x1_ref, route_ref, o_ref, idx_smem, ybuf, idx_sem, row_sem, *, tm):
    i = pl.program_id(0)
    cp = pltpu.make_async_copy(dest_hbm.at[i], idx_smem, idx_sem)
    cp.start()
    cp.wait()

    def row_copy(t, k, d):
        return pltpu.make_async_copy(yb_hbm.at[pl.ds(d, 1)], ybuf.at[k, pl.ds(t, 1)], row_sem)

    def start(t, carry):
        row_copy(t, 0, idx_smem[2 * t]).start(priority=0)
        row_copy(t, 1, idx_smem[2 * t + 1]).start(priority=1)
        return carry

    def wait(t, carry):
        row_copy(0, 0, 0).wait()
        row_copy(0, 1, 0).wait()
        return carry

    lax.fori_loop(0, tm, start, 0, unroll=ROW_DMA_UNROLL)
    lax.fori_loop(0, tm, wait, 0, unroll=ROW_DMA_UNROLL)
    a0, b0 = _unpack_pair(ybuf[0])
    a1, b1 = _unpack_pair(ybuf[1])
    g0 = route_ref[:, 2:3]
    g1 = route_ref[:, 3:4]
    half = a0.shape[1]
    o_ref[:, :half] = x1_ref[:, :half] + (g0 * a0 + g1 * a1)
    o_ref[:, half:] = x1_ref[:, half:] + (g0 * b0 + g1 * b1)


def _combine(dest2d, yb, x1, route, tm):
    n, d = x1.shape
    w = yb.shape[1]
    return pl.pallas_call(
        functools.partial(_combine_kernel, tm=tm),
        grid=(n // tm,),
        in_specs=[
            pl.BlockSpec(memory_space=pl.ANY),
            pl.BlockSpec(memory_space=pl.ANY),
            pl.BlockSpec((tm, d), lambda i: (i, 0)),
            pl.BlockSpec((tm, LANES), lambda i: (i, 0)),
        ],
        out_specs=pl.BlockSpec((tm, d), lambda i: (i, 0)),
        out_shape=jax.ShapeDtypeStruct((n, d), F32),
        scratch_shapes=[pltpu.SMEM((2 * tm,), I32), pltpu.VMEM((2, tm, w), U32),
                        pltpu.SemaphoreType.DMA, pltpu.SemaphoreType.DMA],
        compiler_params=_cparams(1),
        name="moe_combine",
    )(dest2d, yb, x1, route)


def _routing_plan(route, n):
    e_flat = route[:, :2].astype(I32).reshape(-1)
    blk = MOE_BLOCK
    onehot = (e_flat[:, None] == jnp.arange(N_EXPERTS, dtype=I32)[None, :]).astype(BF16).reshape(-1, blk, N_EXPERTS)
    tri = (jnp.arange(blk)[:, None] > jnp.arange(blk)[None, :]).astype(BF16)
    within = jnp.einsum('ij,bjk->bik', tri, onehot, preferred_element_type=F32)
    blk_tot = jnp.sum(onehot.astype(F32), axis=1)
    blk_base = jnp.cumsum(blk_tot, axis=0) - blk_tot
    rank = jnp.sum((within + blk_base[:, None, :]) * onehot.astype(F32), axis=-1).reshape(-1).astype(I32)
    counts = jnp.sum(blk_tot, axis=0).astype(I32)
    padded = (counts + MOE_BLOCK - 1) // MOE_BLOCK * MOE_BLOCK
    pad_end = jnp.cumsum(padded)
    pad_start = pad_end - padded
    dest = pad_start[e_flat] + rank
    cap = 2 * n + N_EXPERTS * MOE_BLOCK
    n_blk = cap // MOE_BLOCK
    blk_pos = jnp.arange(n_blk, dtype=I32) * MOE_BLOCK
    blk_exp = jnp.minimum(jnp.sum((pad_end[None, :] <= blk_pos[:, None]).astype(I32), axis=1), N_EXPERTS - 1)
    n_used = (pad_end[-1:] // MOE_BLOCK).astype(I32)
    return dest.astype(I32), blk_exp, n_used, cap


def _layer(x, mem, g_mix, w_in, qk_gain, na_rpb, t5, g_mem, w_mem_kv, w_out, g_ffn, w_r1, b_r1, w_r2, b_r2, w1, w3, w2):
    b, s, d = x.shape
    n = b * s
    tm = ROW_TILE
    scale = HEAD_DIM ** -0.5
    dils = tuple(dil for _, dil in DIL_PAIRS)

    o_qd, o_kd, o_vd, o_qm = 3 * NA_W, 3 * NA_W + 3 * DIL_W, 3 * NA_W + 6 * DIL_W, 3 * NA_W + 9 * DIL_W
    w_bf = w_in.astype(BF16)
    col_blocks = [w_bf[:, :3 * NA_W], w_bf[:, o_qm:o_qm + MEM_W]]
    for g in range(len(DIL_PAIRS)):
        for base in (o_qd, o_kd, o_vd):
            col_blocks.append(w_bf[:, base + g * DIL_W:base + (g + 1) * DIL_W])
    w_perm = jnp.concatenate(col_blocks, axis=1)
    tile = lambda v: jnp.tile(v.astype(F32), CHUNK // HEAD_DIM)
    ones = jnp.ones((CHUNK,), F32)
    qa_g, ka_g = tile(qk_gain[0, 0]) * scale, tile(qk_gain[0, 1])
    qd_g, kd_g = tile(qk_gain[1, 0]) * scale, tile(qk_gain[1, 1])
    qm_g, km_g = tile(qk_gain[2, 0]) * scale, tile(qk_gain[2, 1])
    main_w = 3 * NA_W + MEM_W
    plan = [(True, 0, 0, 1), (True, 0, CHUNK, 1), (True, 0, 2 * CHUNK, 1), (True, 0, 3 * CHUNK, 1),
            (False, 0, 4 * CHUNK, 1), (False, 0, 5 * CHUNK, 1), (True, 0, 6 * CHUNK, 1)]
    gains = [qa_g, qa_g, ka_g, ka_g, ones, ones, qm_g]
    for gi, dil in enumerate(dils):
        plan += [(True, 1 + gi, 0, dil), (True, 1 + gi, DIL_W, dil), (False, 1 + gi, 2 * DIL_W, dil)]
        gains += [qd_g, kd_g, ones]
    tpb = s // tm
    out_shapes = [jax.ShapeDtypeStruct((n, main_w), BF16)]
    out_specs = [pl.BlockSpec((tm, main_w), lambda i: (i, 0))]
    for dil in dils:
        if dil == 1:
            out_shapes.append(jax.ShapeDtypeStruct((n, 3 * DIL_W), BF16))
            out_specs.append(pl.BlockSpec((tm, 3 * DIL_W), lambda i: (i, 0)))
        else:
            out_shapes.append(jax.ShapeDtypeStruct((b, dil, s // dil, 3 * DIL_W), BF16))
            out_specs.append(pl.BlockSpec((1, dil, tm // dil, 3 * DIL_W), lambda i: (i // tpb, 0, i % tpb, 0)))
    pm, *dgs = _project(x.reshape(n, d), g_mix, w_perm, jnp.stack(gains), tuple(plan), out_shapes, out_specs, tm)

    m_rows = mem.shape[0] * mem.shape[1]
    tmm = min(tm, m_rows)
    kv_plan = ((True, 0, 0, 1), (False, 0, CHUNK, 1))
    kvm = _project(mem.reshape(m_rows, d), g_mem, w_mem_kv.astype(BF16), jnp.stack([km_g, ones]), kv_plan,
                   [jax.ShapeDtypeStruct((m_rows, 2 * MEM_W), BF16)],
                   [pl.BlockSpec((tmm, 2 * MEM_W), lambda i: (i, 0))], tmm)[0]
    kvm = kvm.reshape(mem.shape[0], mem.shape[1], 2 * MEM_W)

    out_na = _neighbourhood_attention(pm.reshape(b, s, main_w), _na_bias_table(na_rpb), b, s)

    t5g = t5.reshape(T5_BUCKETS, len(DIL_PAIRS), DIL_HEADS_PER_GROUP)
    dil_outs = []
    for gi, (dg, dil) in enumerate(zip(dgs, dils)):
        dg = dg.reshape(b, dil, s // dil, 3 * DIL_W)
        dil_outs.append(_dilated_attention(dg, _dil_band_table(t5g, gi, dil), b, dil, s // dil))

    w_r = jnp.zeros((d, LANES), F32).at[:, :N_GROUPS].set(w_r1.astype(F32)).at[:, N_GROUPS:N_GROUPS + N_EXPERTS].set(w_r2.astype(F32))
    b_r = jnp.zeros((1, LANES), F32).at[0, :N_GROUPS].set(b_r1.astype(F32)).at[0, N_GROUPS:N_GROUPS + N_EXPERTS].set(b_r2.astype(F32))
    x1, h_packed, route = _mix(x.reshape(n, d), out_na.reshape(n, NA_W), pm, 3 * NA_W, dil_outs, kvm,
                               w_out.astype(BF16), g_ffn, w_r, b_r, s)

    dest, blk_exp, n_used, cap = _routing_plan(route, n)
    tmd = 256
    dest2d = dest.reshape(n // tmd, 2 * tmd)
    xs = _dispatch(dest2d, h_packed, cap, tmd)
    yb = _experts(blk_exp, n_used, xs, w1.astype(BF16), w3.astype(BF16), w2.astype(BF16))
    out = _combine(dest2d, yb, x1, route, tmd)
    return out.reshape(b, s, d)


def kernel(x, mem, g_mix, w_in, qk_gain, na_rpb, t5_table, g_mem, w_mem_kv, w_out, g_ffn, w_r1, b_r1, w_r2, b_r2, w1, w3, w2):
    for l in range(g_mix.shape[0]):
        x = _layer(x, mem, g_mix[l], w_in[l], qk_gain[l], na_rpb[l], t5_table, g_mem[l], w_mem_kv[l], w_out[l],
                   g_ffn[l], w_r1[l], b_r1[l], w_r2[l], b_r2[l], w1[l], w3[l], w2[l])
    return x
```

```python
import functools
import math

import jax
import jax.numpy as jnp
from jax import lax
from jax.experimental import pallas as pl
from jax.experimental.pallas import tpu as pltpu

F32 = jnp.float32
BF16 = jnp.bfloat16
U32 = jnp.uint32
I32 = jnp.int32

HEAD_DIM = 64
GRID_W = 64
NA_HEADS = 8
NA_WIN_ROWS = 8
NA_WIN_COLS = 16
NA_GROUP_ROWS = 4
NA_KEY_ROWS = NA_WIN_ROWS + NA_GROUP_ROWS - 1
DIL_PAIRS = ((128, 1), (512, 4), (2048, 16))
DIL_HEADS_PER_GROUP = 4
DIL_SIDE = 64
DIL_BLOCK = 128
MEM_HEADS = 4
T5_BUCKETS = 32
T5_MAX_DIST = 1024
N_GROUPS = 4
EXPERTS_PER_GROUP = 8
N_EXPERTS = N_GROUPS * EXPERTS_PER_GROUP
MOE_BLOCK = 256
EPS = 1e-6
NEG_INF = -1e30

NA_W = NA_HEADS * HEAD_DIM
DIL_W = DIL_HEADS_PER_GROUP * HEAD_DIM
MEM_W = MEM_HEADS * HEAD_DIM
LANES = 128
CHUNK = 256
ROW_TILE = 512
VMEM_LIMIT = 48 * 1024 * 1024
MIX_SUBTILES = 2
ROW_DMA_UNROLL = 8


def _cparams(n_axes):
    return pltpu.CompilerParams(dimension_semantics=("arbitrary",) * n_axes, vmem_limit_bytes=VMEM_LIMIT)


def _nt_dot(a, b):
    return lax.dot_general(a, b, (((1,), (1,)), ((), ())), preferred_element_type=F32)


def _pack_pair(a, b):
    ua = lax.bitcast_convert_type(a.astype(BF16).astype(F32), U32)
    ub = lax.bitcast_convert_type(b.astype(BF16).astype(F32), U32)
    return (ua >> 16) | (ub & jnp.uint32(0xFFFF0000))


def _unpack_pair(p):
    a = lax.bitcast_convert_type(p << 16, F32)
    b = lax.bitcast_convert_type(p & jnp.uint32(0xFFFF0000), F32)
    return a, b


def _proj_kernel(x_ref, g_ref, w_ref, gain_ref, bd_ref, *refs, plan, tm):
    n_buf = CHUNK // LANES
    out_refs, ybufs = refs[:-n_buf], refs[-n_buf:]
    x = x_ref[...]
    ms = jnp.mean(x * x, axis=-1, keepdims=True)
    h = (x * lax.rsqrt(ms + EPS) * g_ref[...]).astype(BF16)
    for c, (normed, oi, col, dil) in enumerate(plan):
        y = jnp.dot(h, w_ref[:, c * CHUNK:(c + 1) * CHUNK], preferred_element_type=F32)
        if normed:
            msq = jnp.dot((y * y).astype(BF16), bd_ref[...], preferred_element_type=F32)
            y = y * lax.rsqrt(msq + EPS) * gain_ref[c:c + 1, :]
        o_ref = out_refs[oi]
        if dil == 1:
            o_ref[:, col:col + CHUNK] = y.astype(o_ref.dtype)
        else:
            for j, ybuf in enumerate(ybufs):
                ybuf[...] = y[:, j * LANES:(j + 1) * LANES]
            for rho in range(dil):
                for j, ybuf in enumerate(ybufs):
                    rows = ybuf[pl.ds(rho, tm // dil, stride=dil), :]
                    o_ref[0, rho, :, col + j * LANES:col + (j + 1) * LANES] = rows.astype(o_ref.dtype)


def _project(x2d, g, w_bf16, gains, plan, out_shapes, out_specs, tm):
    n, d = x2d.shape
    cols = w_bf16.shape[1]
    bd = jnp.kron(jnp.eye(CHUNK // HEAD_DIM, dtype=F32), jnp.full((HEAD_DIM, HEAD_DIM), 1.0 / HEAD_DIM, F32)).astype(BF16)
    return pl.pallas_call(
        functools.partial(_proj_kernel, plan=plan, tm=tm),
        grid=(n // tm,),
        in_specs=[
            pl.BlockSpec((tm, d), lambda i: (i, 0)),
            pl.BlockSpec((1, d), lambda i: (0, 0)),
            pl.BlockSpec((d, cols), lambda i: (0, 0)),
            pl.BlockSpec(gains.shape, lambda i: (0, 0)),
            pl.BlockSpec((CHUNK, CHUNK), lambda i: (0, 0)),
        ],
        out_specs=out_specs,
        out_shape=out_shapes,
        scratch_shapes=[pltpu.VMEM((tm, LANES), F32)] * (CHUNK // LANES),
        compiler_params=_cparams(1),
        name="rmsnorm_project",
    )(x2d, g.reshape(1, d), w_bf16, gains, bd)


def _na_kernel(q_ref, k_ref, v_ref, bias_ref, o_ref, *, groups_per_step, n_rows):
    step = pl.program_id(2)
    n_groups = n_rows // NA_GROUP_ROWS
    tq = NA_GROUP_ROWS * GRID_W
    keys = NA_KEY_ROWS * GRID_W
    for i in range(groups_per_step):
        rg = step * groups_per_step + i
        key_row0 = jnp.clip(rg * NA_GROUP_ROWS - NA_WIN_ROWS // 2, 0, n_rows - NA_KEY_ROWS)
        gtype = jnp.where(rg == 0, 0, jnp.where(rg == n_groups - 1, 2, 1))
        k0 = pl.multiple_of(key_row0 * GRID_W, GRID_W)
        q = q_ref[0, i * tq:(i + 1) * tq, :]
        k = k_ref[0, pl.ds(k0, keys), :]
        v = v_ref[0, pl.ds(k0, keys), :]
        outs = []
        for hh in range(LANES // HEAD_DIM):
            sl = slice(hh * HEAD_DIM, (hh + 1) * HEAD_DIM)
            s = _nt_dot(q[:, sl], k[:, sl]) + bias_ref[hh, gtype]
            m = jnp.max(s, axis=-1, keepdims=True)
            e = jnp.exp(s - m)
            den = jnp.sum(e, axis=-1, keepdims=True)
            o = jnp.dot(e.astype(BF16), v[:, sl], preferred_element_type=F32)
            outs.append(o / den)
        o_ref[0, i * tq:(i + 1) * tq, :] = jnp.concatenate(outs, axis=-1).astype(o_ref.dtype)


def _toeplitz(g, rows, cols):
    p = g.shape[-1]
    tiled = jnp.tile(g, (1,) * (g.ndim - 1) + (rows,))[..., :rows * (p - 1)]
    return tiled.reshape(g.shape[:-1] + (rows, p - 1))[..., :cols]


def _na_bias_table(rpb):
    kh, kw = NA_WIN_ROWS, NA_WIN_COLS
    h = rpb.shape[0]
    rpb = rpb.astype(F32)
    by_row = jnp.stack([rpb[:, kh - 1 - t:2 * kh - 1 - t, :] for t in range(kh)], axis=1)
    lo = GRID_W - kw
    padded = jnp.pad(by_row, ((0, 0), (0, 0), (0, 0), (lo, 2 * GRID_W - lo - (2 * kw - 1))), constant_values=NEG_INF)
    vals = _toeplitz(jnp.roll(padded, -(GRID_W - 1), axis=-1), GRID_W, GRID_W)
    c = jnp.arange(GRID_W)[:, None]
    kc = jnp.arange(GRID_W)[None, :]
    c0 = jnp.clip(c - kw // 2, 0, GRID_W - kw)
    inside = (kc >= c0) & (kc < c0 + kw)
    per_row = jnp.where(inside, vals, NEG_INF).transpose(0, 1, 3, 2, 4)

    def placed(t, shift):
        return jnp.pad(per_row[:, t], ((0, 0), (0, 0), (shift, NA_KEY_ROWS - kh - shift), (0, 0)), constant_values=NEG_INF)

    mid = kh // 2
    top = jnp.stack([placed(i, 0) for i in range(NA_GROUP_ROWS)], axis=1)
    interior = jnp.stack([placed(mid, i) for i in range(NA_GROUP_ROWS)], axis=1)
    bottom = jnp.stack([placed(mid + i, NA_KEY_ROWS - kh) for i in range(NA_GROUP_ROWS)], axis=1)
    table = jnp.stack([top, interior, bottom], axis=1)
    return table.reshape(h, 3, NA_GROUP_ROWS * GRID_W, NA_KEY_ROWS * GRID_W)


def _neighbourhood_attention(pm3, bias_tab, b, s):
    n_rows = s // GRID_W
    groups_per_step = 2
    tq = groups_per_step * NA_GROUP_ROWS * GRID_W
    pairs = NA_W // LANES
    assert n_rows % (groups_per_step * NA_GROUP_ROWS) == 0 and n_rows >= NA_KEY_ROWS + NA_GROUP_ROWS
    return pl.pallas_call(
        functools.partial(_na_kernel, groups_per_step=groups_per_step, n_rows=n_rows),
        grid=(b, pairs, n_rows // (groups_per_step * NA_GROUP_ROWS)),
        in_specs=[
            pl.BlockSpec((1, tq, LANES), lambda bi, hp, rb: (bi, rb, hp)),
            pl.BlockSpec((1, s, LANES), lambda bi, hp, rb: (bi, 0, pairs + hp)),
            pl.BlockSpec((1, s, LANES), lambda bi, hp, rb: (bi, 0, 2 * pairs + hp)),
            pl.BlockSpec((LANES // HEAD_DIM, 3, NA_GROUP_ROWS * GRID_W, NA_KEY_ROWS * GRID_W), lambda bi, hp, rb: (hp, 0, 0, 0)),
        ],
        out_specs=pl.BlockSpec((1, tq, LANES), lambda bi, hp, rb: (bi, rb, hp)),
        out_shape=jax.ShapeDtypeStruct((b, s, NA_W), BF16),
        compiler_params=_cparams(3),
        name="neighbourhood_attention",
    )(pm3, pm3, pm3, bias_tab)


def _dil_kernel(q_ref, k_ref, v_ref, band_ref, o_ref, lse_ref, *, length):
    nblk = length // DIL_BLOCK
    span = DIL_BLOCK + 2 * DIL_SIDE

    def blk(i, carry):
        i0 = pl.multiple_of(i * DIL_BLOCK, DIL_BLOCK)
        p0 = pl.multiple_of(jnp.maximum(i0 - DIL_SIDE, 0), DIL_SIDE)
        n0 = pl.multiple_of(jnp.minimum(i0 + DIL_BLOCK, length - DIL_SIDE), DIL_SIDE)
        q = q_ref[0, 0, pl.ds(i0, DIL_BLOCK), :]
        k = jnp.concatenate([k_ref[0, 0, pl.ds(p0, DIL_SIDE), :], k_ref[0, 0, pl.ds(i0, DIL_BLOCK), :],
                             k_ref[0, 0, pl.ds(n0, DIL_SIDE), :]], axis=0)
        v = jnp.concatenate([v_ref[0, 0, pl.ds(p0, DIL_SIDE), :], v_ref[0, 0, pl.ds(i0, DIL_BLOCK), :],
                             v_ref[0, 0, pl.ds(n0, DIL_SIDE), :]], axis=0)
        kpos = i0 - DIL_SIDE + lax.broadcasted_iota(I32, (1, span), 1)
        valid = (kpos >= 0) & (kpos < length)
        outs, lses = [], []
        for hh in range(LANES // HEAD_DIM):
            sl = slice(hh * HEAD_DIM, (hh + 1) * HEAD_DIM)
            s = _nt_dot(q[:, sl], k[:, sl]) + band_ref[hh]
            s = jnp.where(valid, s, NEG_INF)
            m = jnp.max(s, axis=-1, keepdims=True)
            e = jnp.exp(s - m)
            den = jnp.sum(e, axis=-1, keepdims=True)
            o = jnp.dot(e.astype(BF16), v[:, sl], preferred_element_type=F32)
            outs.append(o / den)
            lses.append(jnp.broadcast_to(m + jnp.log(den), (DIL_BLOCK, HEAD_DIM)))
        o_ref[0, 0, pl.ds(i0, DIL_BLOCK), :] = jnp.concatenate(outs, axis=-1)
        lse_ref[0, 0, pl.ds(i0, DIL_BLOCK), :] = jnp.concatenate(lses, axis=-1)
        return carry

    unroll = 2 if nblk % 2 == 0 else 1

    def blk_group(j, carry):
        for u in range(unroll):
            blk(j * unroll + u, carry)
        return carry

    lax.fori_loop(0, nblk // unroll, blk_group, 0)


def _t5_bucket(rel):
    nb = T5_BUCKETS // 2
    max_exact = nb // 2
    n = jnp.abs(rel)
    upper = (rel > 0).astype(I32) * nb
    nf = jnp.maximum(n, 1).astype(F32)
    large = max_exact + (jnp.log(nf / max_exact) / math.log(T5_MAX_DIST / max_exact) * (nb - max_exact)).astype(I32)
    large = jnp.minimum(large, nb - 1)
    return upper + jnp.where(n < max_exact, n, large)


def _dil_band_table(t5, g, dil):
    offs = jnp.arange(-DIL_SIDE, DIL_SIDE + 1) * dil
    bias = t5[_t5_bucket(offs), g].T.astype(F32)
    span = DIL_BLOCK + 2 * DIL_SIDE
    period = DIL_BLOCK + span
    g_vec = jnp.pad(bias, ((0, 0), (0, period - bias.shape[1])), constant_values=NEG_INF)
    return _toeplitz(g_vec, DIL_BLOCK, span)


def _dilated_attention(dg, band, b, dil, length):
    pairs = DIL_W // LANES
    spec = lambda off: pl.BlockSpec((1, 1, length, LANES), lambda bi, rho, hp: (bi, rho, 0, off + hp))
    out_spec = pl.BlockSpec((1, 1, length, LANES), lambda bi, rho, hp: (bi, rho, 0, hp))
    shape = jax.ShapeDtypeStruct((b, dil, length, DIL_W), F32)
    return pl.pallas_call(
        functools.partial(_dil_kernel, length=length),
        grid=(b, dil, pairs),
        in_specs=[spec(0), spec(pairs), spec(2 * pairs),
                  pl.BlockSpec((LANES // HEAD_DIM, DIL_BLOCK, DIL_BLOCK + 2 * DIL_SIDE), lambda bi, rho, hp: (hp, 0, 0))],
        out_specs=(out_spec, out_spec),
        out_shape=(shape, shape),
        compiler_params=_cparams(3),
        name=f"dilated_attention_d{dil}",
    )(dg, dg, dg, band)


def _mix_kernel(x_ref, na_ref, qm_ref, o0_ref, l0_ref, o1_ref, l1_ref, o2_ref, l2_ref, kvm_ref, wout_ref, gffn_ref,
                wrh_ref, wrl_ref, br_ref, x1_ref, hp_ref, route_ref, *ibufs, tm, dils):
    sub = tm // MIX_SUBTILES
    bufs = iter(ibufs)
    for si in range(MIX_SUBTILES):
        _mix_rows(si, sub, bufs, x_ref, na_ref, qm_ref, (o0_ref, o1_ref, o2_ref), (l0_ref, l1_ref, l2_ref), kvm_ref,
                  wout_ref, gffn_ref, wrh_ref, wrl_ref, br_ref, x1_ref, hp_ref, route_ref, dils)


def _mix_rows(si, sub, bufs, x_ref, na_ref, qm_ref, o_refs, l_refs, kvm_ref, wout_ref, gffn_ref, wrh_ref, wrl_ref,
              br_ref, x1_ref, hp_ref, route_ref, dils):
    rows = slice(si * sub, (si + 1) * sub)

    def token_major(ref, dil):
        if dil == 1:
            return ref[0, 0, rows, :]
        n_cls = sub // dil
        mine = [next(bufs) for _ in range(DIL_W // LANES)]
        for rho in range(dil):
            for j, ibuf in enumerate(mine):
                ibuf[pl.ds(rho, n_cls, stride=dil), :] = ref[0, rho, si * n_cls:(si + 1) * n_cls, j * LANES:(j + 1) * LANES]
        return jnp.concatenate([ibuf[...] for ibuf in mine], axis=-1)

    lses = [token_major(l_ref, dil) for l_ref, dil in zip(l_refs, dils)]
    lmax = jnp.maximum(jnp.maximum(lses[0], lses[1]), lses[2])
    wts = [jnp.exp(l - lmax) for l in lses]
    num = None
    for o_ref, dil, w in zip(o_refs, dils, wts):
        term = w * token_major(o_ref, dil)
        num = term if num is None else num + term
    out_dil = num / (wts[0] + wts[1] + wts[2])

    qm = qm_ref[rows, :]
    mem_outs = []
    for hh in range(MEM_HEADS):
        sl = slice(hh * HEAD_DIM, (hh + 1) * HEAD_DIM)
        s = _nt_dot(qm[:, sl], kvm_ref[0, :, sl])
        m = jnp.max(s, axis=-1, keepdims=True)
        e = jnp.exp(s - m)
        den = jnp.sum(e, axis=-1, keepdims=True)
        vm = kvm_ref[0, :, MEM_W + hh * HEAD_DIM:MEM_W + (hh + 1) * HEAD_DIM]
        mem_outs.append(jnp.dot(e.astype(BF16), vm, preferred_element_type=F32) / den)
    out_mem = jnp.concatenate(mem_outs, axis=-1)

    y = jnp.dot(na_ref[rows, :], wout_ref[0:NA_W, :], preferred_element_type=F32)
    y = y + jnp.dot(out_dil.astype(BF16), wout_ref[NA_W:NA_W + DIL_W, :], preferred_element_type=F32)
    y = y + jnp.dot(out_mem.astype(BF16), wout_ref[NA_W + DIL_W:, :], preferred_element_type=F32)
    x1 = x_ref[rows, :] + y
    x1_ref[rows, :] = x1

    ms = jnp.mean(x1 * x1, axis=-1, keepdims=True)
    h = x1 * lax.rsqrt(ms + EPS) * gffn_ref[...]
    half = h.shape[1] // 2
    hp_ref[rows, :] = _pack_pair(h[:, :half], h[:, half:])
    h_hi = h.astype(BF16)
    h_lo = (h - h_hi.astype(F32)).astype(BF16)
    logits = (jnp.dot(h_hi, wrh_ref[...], preferred_element_type=F32)
              + (jnp.dot(h_hi, wrl_ref[...], preferred_element_type=F32)
                 + jnp.dot(h_lo, wrh_ref[...], preferred_element_type=F32))) + br_ref[...]
    lane = lax.broadcasted_iota(I32, logits.shape, 1)
    neg = -jnp.inf
    gl = jnp.where(lane < N_GROUPS, logits, neg)
    gmax = jnp.max(gl, axis=-1, keepdims=True)
    gidx = jnp.min(jnp.where(gl == gmax, lane, LANES), axis=-1, keepdims=True)
    grp_gate = 1.0 / jnp.sum(jnp.where(lane < N_GROUPS, jnp.exp(logits - gmax), 0.0), axis=-1, keepdims=True)
    lo = N_GROUPS + EXPERTS_PER_GROUP * gidx
    fl = jnp.where((lane >= lo) & (lane < lo + EXPERTS_PER_GROUP), logits, neg)
    v1 = jnp.max(fl, axis=-1, keepdims=True)
    i1 = jnp.min(jnp.where(fl == v1, lane, LANES), axis=-1, keepdims=True)
    fl2 = jnp.where(lane == i1, neg, fl)
    v2 = jnp.max(fl2, axis=-1, keepdims=True)
    i2 = jnp.min(jnp.where(fl2 == v2, lane, LANES), axis=-1, keepdims=True)
    t = jnp.exp(v2 - v1)
    g1 = grp_gate / (1.0 + t)
    g2 = grp_gate * t / (1.0 + t)
    e1 = (i1 - N_GROUPS).astype(F32)
    e2 = (i2 - N_GROUPS).astype(F32)
    route_ref[rows, :] = jnp.where(lane == 0, e1, jnp.where(lane == 1, e2, jnp.where(lane == 2, g1, jnp.where(lane == 3, g2, 0.0))))


def _mix(x2d, na, pm, qm_col, dil_outs, kvm, w_out_bf16, g_ffn, w_r, b_r, s):
    n, d = x2d.shape
    tm = ROW_TILE
    tpb = s // tm
    dils = tuple(dil for _, dil in DIL_PAIRS)
    w_r_hi = w_r.astype(BF16)
    w_r_lo = (w_r - w_r_hi.astype(F32)).astype(BF16)
    dil_specs, dil_args = [], []
    for (o, lse), dil in zip(dil_outs, dils):
        spec = pl.BlockSpec((1, dil, tm // dil, DIL_W), lambda i: (i // tpb, 0, i % tpb, 0))
        dil_specs += [spec, spec]
        dil_args += [o, lse]
    return pl.pallas_call(
        functools.partial(_mix_kernel, tm=tm, dils=dils),
        grid=(n // tm,),
        in_specs=[
            pl.BlockSpec((tm, d), lambda i: (i, 0)),
            pl.BlockSpec((tm, NA_W), lambda i: (i, 0)),
            pl.BlockSpec((tm, MEM_W), lambda i: (i, qm_col // MEM_W)),
            *dil_specs,
            pl.BlockSpec((1, kvm.shape[1], kvm.shape[2]), lambda i: (i // tpb, 0, 0)),
            pl.BlockSpec(w_out_bf16.shape, lambda i: (0, 0)),
            pl.BlockSpec((1, d), lambda i: (0, 0)),
            pl.BlockSpec(w_r.shape, lambda i: (0, 0)),
            pl.BlockSpec(w_r.shape, lambda i: (0, 0)),
            pl.BlockSpec(b_r.shape, lambda i: (0, 0)),
        ],
        out_specs=(
            pl.BlockSpec((tm, d), lambda i: (i, 0)),
            pl.BlockSpec((tm, d // 2), lambda i: (i, 0)),
            pl.BlockSpec((tm, LANES), lambda i: (i, 0)),
        ),
        out_shape=(
            jax.ShapeDtypeStruct((n, d), F32),
            jax.ShapeDtypeStruct((n, d // 2), U32),
            jax.ShapeDtypeStruct((n, LANES), F32),
        ),
        scratch_shapes=[pltpu.VMEM((tm // MIX_SUBTILES, LANES), F32)]
        * (MIX_SUBTILES * 2 * sum(dil > 1 for dil in dils) * (DIL_W // LANES)),
        compiler_params=_cparams(1),
        name="merge_memattn_outproj_router",
    )(x2d, na, pm, *dil_args, kvm, w_out_bf16, g_ffn.reshape(1, d), w_r_hi, w_r_lo, b_r)


def _dispatch_kernel(dest_hbm, h_hbm, xs_init_hbm, xs_hbm, idx_smem, idx_sem, row_sem, *, tm):
    del xs_init_hbm
    i = pl.program_id(0)
    n_steps = pl.num_programs(0)
    slot = i % 2

    def idx_copy(step, sl):
        return pltpu.make_async_copy(dest_hbm.at[step], idx_smem.at[sl], idx_sem.at[sl])

    def row_copy(t, d):
        return pltpu.make_async_copy(h_hbm.at[pl.ds(t, 1)], xs_hbm.at[pl.ds(d, 1)], row_sem)

    @pl.when(i == 0)
    def _():
        idx_copy(0, 0).start()

    idx_copy(i, slot).wait()

    @pl.when(i + 1 < n_steps)
    def _():
        idx_copy(i + 1, 1 - slot).start()

    base = i * tm

    def start(t, carry):
        row_copy(base + t, idx_smem[slot, 2 * t]).start(priority=0)
        row_copy(base + t, idx_smem[slot, 2 * t + 1]).start(priority=1)
        return carry

    def wait(t, carry):
        row_copy(0, 0).wait()
        row_copy(0, 0).wait()
        return carry

    lax.fori_loop(0, tm, start, 0, unroll=ROW_DMA_UNROLL)

    @pl.when(i > 0)
    def _():
        lax.fori_loop(0, tm, wait, 0, unroll=ROW_DMA_UNROLL)

    @pl.when(i == n_steps - 1)
    def _():
        lax.fori_loop(0, tm, wait, 0, unroll=ROW_DMA_UNROLL)


def _dispatch(dest2d, h_packed, cap, tm):
    n, w = h_packed.shape
    xs_init = jnp.zeros((cap, w), U32)
    return pl.pallas_call(
        functools.partial(_dispatch_kernel, tm=tm),
        grid=(n // tm,),
        in_specs=[
            pl.BlockSpec(memory_space=pl.ANY),
            pl.BlockSpec(memory_space=pl.ANY),
            pl.BlockSpec(memory_space=pl.ANY),
        ],
        out_specs=pl.BlockSpec(memory_space=pl.ANY),
        out_shape=jax.ShapeDtypeStruct((cap, w), U32),
        scratch_shapes=[pltpu.SMEM((2, 2 * tm), I32), pltpu.SemaphoreType.DMA((2,)), pltpu.SemaphoreType.DMA],
        input_output_aliases={2: 0},
        compiler_params=_cparams(1),
        name="moe_dispatch",
    )(dest2d, h_packed, xs_init)


def _expert_kernel(blk_exp_ref, n_used_ref, xs_ref, w1_ref, w3_ref, w2_ref, yb_ref):
    del blk_exp_ref

    @pl.when(pl.program_id(0) < n_used_ref[0])
    def _():
        a, b = _unpack_pair(xs_ref[...])
        a, b = a.astype(BF16), b.astype(BF16)
        half = a.shape[1]
        h1 = (jnp.dot(a, w1_ref[0, :half, :], preferred_element_type=F32)
              + jnp.dot(b, w1_ref[0, half:, :], preferred_element_type=F32))
        h3 = (jnp.dot(a, w3_ref[0, :half, :], preferred_element_type=F32)
              + jnp.dot(b, w3_ref[0, half:, :], preferred_element_type=F32))
        act = (h1 * jax.nn.sigmoid(h1) * h3).astype(BF16)
        y = jnp.dot(act, w2_ref[0], preferred_element_type=F32)
        yb_ref[...] = _pack_pair(y[:, :half], y[:, half:])

    @pl.when(pl.program_id(0) >= n_used_ref[0])
    def _():
        yb_ref[...] = jnp.zeros(yb_ref.shape, yb_ref.dtype)


def _experts(blk_exp, n_used, xs, w1, w3, w2):
    cap, w = xs.shape
    n_blk = cap // MOE_BLOCK
    d, de = w1.shape[1], w1.shape[2]
    row = lambda i, be, nu: (jnp.minimum(i, nu[0] - 1), 0)
    wsel = lambda i, be, nu: (be[jnp.minimum(i, nu[0] - 1)], 0, 0)
    return pl.pallas_call(
        _expert_kernel,
        grid_spec=pltpu.PrefetchScalarGridSpec(
            num_scalar_prefetch=2,
            grid=(n_blk,),
            in_specs=[
                pl.BlockSpec((MOE_BLOCK, w), row),
                pl.BlockSpec((1, d, de), wsel),
                pl.BlockSpec((1, d, de), wsel),
                pl.BlockSpec((1, de, d), wsel),
            ],
            out_specs=pl.BlockSpec((MOE_BLOCK, w), lambda i, be, nu: (i, 0)),
        ),
        out_shape=jax.ShapeDtypeStruct((cap, w), U32),
        compiler_params=_cparams(1),
        name="moe_experts",
    )(blk_exp, n_used, xs, w1, w3, w2)


def _combine_kernel(dest_hbm, yb_hbm, x1_ref, route_ref, o_ref, idx_smem, ybuf, idx_sem, row_sem, *, tm):
    i = pl.program_id(0)
    n_steps = pl.num_programs(0)
    slot = i % 2

    def idx_copy(step, sl):
        return pltpu.make_async_copy(dest_hbm.at[step], idx_smem.at[sl], idx_sem.at[sl])

    def row_copy(sl, t, k, d):
        return pltpu.make_async_copy(yb_hbm.at[pl.ds(d, 1)], ybuf.at[sl, k, pl.ds(t, 1)], row_sem.at[sl])

    def issue(sl):
        def start(t, carry):
            row_copy(sl, t, 0, idx_smem[sl, 2 * t]).start(priority=0)
            row_copy(sl, t, 1, idx_smem[sl, 2 * t + 1]).start(priority=1)
            return carry

        lax.fori_loop(0, tm, start, 0, unroll=ROW_DMA_UNROLL)

    @pl.when(i == 0)
    def _():
        idx_copy(0, 0).start()
        idx_copy(0, 0).wait()
        issue(0)

        @pl.when(n_steps > 1)
        def _():
            idx_copy(1, 1).start()

    @pl.when(i + 1 < n_steps)
    def _():
        idx_copy(i + 1, 1 - slot).wait()
        issue(1 - slot)

    @pl.when(i + 2 < n_steps)
    def _():
        idx_copy(i + 2, slot).start()

    def wait(t, carry):
        row_copy(slot, 0, 0, 0).wait()
        row_copy(slot, 0, 1, 0).wait()
        return carry

    lax.fori_loop(0, tm, wait, 0, unroll=ROW_DMA_UNROLL)
    a0, b0 = _unpack_pair(ybuf[slot, 0])
    a1, b1 = _unpack_pair(ybuf[slot, 1])
    g0 = route_ref[:, 2:3]
    g1 = route_ref[:, 3:4]
    half = a0.shape[1]
    o_ref[:, :half] = x1_ref[:, :half] + (g0 * a0 + g1 * a1)
    o_ref[:, half:] = x1_ref[:, half:] + (g0 * b0 + g1 * b1)


def _combine(dest2d, yb, x1, route, tm):
    n, d = x1.shape
    w = yb.shape[1]
    return pl.pallas_call(
        functools.partial(_combine_kernel, tm=tm),
        grid=(n // tm,),
        in_specs=[
            pl.BlockSpec(memory_space=pl.ANY),
            pl.BlockSpec(memory_space=pl.ANY),
            pl.BlockSpec((tm, d), lambda i: (i, 0)),
            pl.BlockSpec((tm, LANES), lambda i: (i, 0)),
        ],
        out_specs=pl.BlockSpec((tm, d), lambda i: (i, 0)),
        out_shape=jax.ShapeDtypeStruct((n, d), F32),
        scratch_shapes=[pltpu.SMEM((2, 2 * tm), I32), pltpu.VMEM((2, 2, tm, w), U32),
                        pltpu.SemaphoreType.DMA((2,)), pltpu.SemaphoreType.DMA((2,))],
        compiler_params=_cparams(1),
        name="moe_combine",
    )(dest2d, yb, x1, route)


def _routing_plan(route, n):
    e_flat = route[:, :2].astype(I32).reshape(-1)
    blk = MOE_BLOCK
    onehot = (e_flat[:, None] == jnp.arange(N_EXPERTS, dtype=I32)[None, :]).astype(BF16).reshape(-1, blk, N_EXPERTS)
    tri = (jnp.arange(blk)[:, None] > jnp.arange(blk)[None, :]).astype(BF16)
    within = jnp.einsum('ij,bjk->bik', tri, onehot, preferred_element_type=F32)
    blk_tot = jnp.sum(onehot.astype(F32), axis=1)
    blk_base = jnp.cumsum(blk_tot, axis=0) - blk_tot
    rank = jnp.sum((within + blk_base[:, None, :]) * onehot.astype(F32), axis=-1).reshape(-1).astype(I32)
    counts = jnp.sum(blk_tot, axis=0).astype(I32)
    padded = (counts + MOE_BLOCK - 1) // MOE_BLOCK * MOE_BLOCK
    pad_end = jnp.cumsum(padded)
    pad_start = pad_end - padded
    dest = pad_start[e_flat] + rank
    cap = 2 * n + N_EXPERTS * MOE_BLOCK
    n_blk = cap // MOE_BLOCK
    blk_pos = jnp.arange(n_blk, dtype=I32) * MOE_BLOCK
    blk_exp = jnp.minimum(jnp.sum((pad_end[None, :] <= blk_pos[:, None]).astype(I32), axis=1), N_EXPERTS - 1)
    n_used = (pad_end[-1:] // MOE_BLOCK).astype(I32)
    return dest.astype(I32), blk_exp, n_used, cap


def _layer(x, mem, g_mix, w_in, qk_gain, na_rpb, t5, g_mem, w_mem_kv, w_out, g_ffn, w_r1, b_r1, w_r2, b_r2, w1, w3, w2):
    b, s, d = x.shape
    n = b * s
    tm = ROW_TILE
    scale = HEAD_DIM ** -0.5
    dils = tuple(dil for _, dil in DIL_PAIRS)

    o_qd, o_kd, o_vd, o_qm = 3 * NA_W, 3 * NA_W + 3 * DIL_W, 3 * NA_W + 6 * DIL_W, 3 * NA_W + 9 * DIL_W
    w_bf = w_in.astype(BF16)
    col_blocks = [w_bf[:, :3 * NA_W], w_bf[:, o_qm:o_qm + MEM_W]]
    for g in range(len(DIL_PAIRS)):
        for base in (o_qd, o_kd, o_vd):
            col_blocks.append(w_bf[:, base + g * DIL_W:base + (g + 1) * DIL_W])
    w_perm = jnp.concatenate(col_blocks, axis=1)
    tile = lambda v: jnp.tile(v.astype(F32), CHUNK // HEAD_DIM)
    ones = jnp.ones((CHUNK,), F32)
    qa_g, ka_g = tile(qk_gain[0, 0]) * scale, tile(qk_gain[0, 1])
    qd_g, kd_g = tile(qk_gain[1, 0]) * scale, tile(qk_gain[1, 1])
    qm_g, km_g = tile(qk_gain[2, 0]) * scale, tile(qk_gain[2, 1])
    main_w = 3 * NA_W + MEM_W
    plan = [(True, 0, 0, 1), (True, 0, CHUNK, 1), (True, 0, 2 * CHUNK, 1), (True, 0, 3 * CHUNK, 1),
            (False, 0, 4 * CHUNK, 1), (False, 0, 5 * CHUNK, 1), (True, 0, 6 * CHUNK, 1)]
    gains = [qa_g, qa_g, ka_g, ka_g, ones, ones, qm_g]
    for gi, dil in enumerate(dils):
        plan += [(True, 1 + gi, 0, dil), (True, 1 + gi, DIL_W, dil), (False, 1 + gi, 2 * DIL_W, dil)]
        gains += [qd_g, kd_g, ones]
    tpb = s // tm
    out_shapes = [jax.ShapeDtypeStruct((n, main_w), BF16)]
    out_specs = [pl.BlockSpec((tm, main_w), lambda i: (i, 0))]
    for dil in dils:
        if dil == 1:
            out_shapes.append(jax.ShapeDtypeStruct((n, 3 * DIL_W), BF16))
            out_specs.append(pl.BlockSpec((tm, 3 * DIL_W), lambda i: (i, 0)))
        else:
            out_shapes.append(jax.ShapeDtypeStruct((b, dil, s // dil, 3 * DIL_W), BF16))
            out_specs.append(pl.BlockSpec((1, dil, tm // dil, 3 * DIL_W), lambda i: (i // tpb, 0, i % tpb, 0)))
    pm, *dgs = _project(x.reshape(n, d), g_mix, w_perm, jnp.stack(gains), tuple(plan), out_shapes, out_specs, tm)

    m_rows = mem.shape[0] * mem.shape[1]
    tmm = min(tm, m_rows)
    kv_plan = ((True, 0, 0, 1), (False, 0, CHUNK, 1))
    kvm = _project(mem.reshape(m_rows, d), g_mem, w_mem_kv.astype(BF16), jnp.stack([km_g, ones]), kv_plan,
                   [jax.ShapeDtypeStruct((m_rows, 2 * MEM_W), BF16)],
                   [pl.BlockSpec((tmm, 2 * MEM_W), lambda i: (i, 0))], tmm)[0]
    kvm = kvm.reshape(mem.shape[0], mem.shape[1], 2 * MEM_W)

    out_na = _neighbourhood_attention(pm.reshape(b, s, main_w), _na_bias_table(na_rpb), b, s)

    t5g = t5.reshape(T5_BUCKETS, len(DIL_PAIRS), DIL_HEADS_PER_GROUP)
    dil_outs = []
    for gi, (dg, dil) in enumerate(zip(dgs, dils)):
        dg = dg.reshape(b, dil, s // dil, 3 * DIL_W)
        dil_outs.append(_dilated_attention(dg, _dil_band_table(t5g, gi, dil), b, dil, s // dil))

    w_r = jnp.zeros((d, LANES), F32).at[:, :N_GROUPS].set(w_r1.astype(F32)).at[:, N_GROUPS:N_GROUPS + N_EXPERTS].set(w_r2.astype(F32))
    b_r = jnp.zeros((1, LANES), F32).at[0, :N_GROUPS].set(b_r1.astype(F32)).at[0, N_GROUPS:N_GROUPS + N_EXPERTS].set(b_r2.astype(F32))
    x1, h_packed, route = _mix(x.reshape(n, d), out_na.reshape(n, NA_W), pm, 3 * NA_W, dil_outs, kvm,
                               w_out.astype(BF16), g_ffn, w_r, b_r, s)

    dest, blk_exp, n_used, cap = _routing_plan(route, n)
    tmd = 256
    dest2d = dest.reshape(n // tmd, 2 * tmd)
    xs = _dispatch(dest2d, h_packed, cap, tmd)
    yb = _experts(blk_exp, n_used, xs, w1.astype(BF16), w3.astype(BF16), w2.astype(BF16))
    out = _combine(dest2d, yb, x1, route, tmd)
    return out.reshape(b, s, d)


def kernel(x, mem, g_mix, w_in, qk_gain, na_rpb, t5_table, g_mem, w_mem_kv, w_out, g_ffn, w_r1, b_r1, w_r2, b_r2, w1, w3, w2):
    for l in range(g_mix.shape[0]):
        x = _layer(x, mem, g_mix[l], w_in[l], qk_gain[l], na_rpb[l], t5_table, g_mem[l], w_mem_kv[l], w_out[l],
                   g_ffn[l], w_r1[l], b_r1[l], w_r2[l], b_r2[l], w1[l], w3[l], w2[l])
    return x
```

```python
import functools
import math

import jax
import jax.numpy as jnp
from jax import lax
from jax.experimental import pallas as pl
from jax.experimental.pallas import tpu as pltpu

F32 = jnp.float32
BF16 = jnp.bfloat16
U32 = jnp.uint32
I32 = jnp.int32

HEAD_DIM = 64
GRID_W = 64
NA_HEADS = 8
NA_WIN_ROWS = 8
NA_WIN_COLS = 16
NA_GROUP_ROWS = 4
NA_KEY_ROWS = NA_WIN_ROWS + NA_GROUP_ROWS - 1
DIL_PAIRS = ((128, 1), (512, 4), (2048, 16))
DIL_HEADS_PER_GROUP = 4
DIL_SIDE = 64
DIL_BLOCK = 128
DIL_UNROLL = 4
MEM_HEADS = 4
T5_BUCKETS = 32
T5_MAX_DIST = 1024
N_GROUPS = 4
EXPERTS_PER_GROUP = 8
N_EXPERTS = N_GROUPS * EXPERTS_PER_GROUP
MOE_BLOCK = 256
EPS = 1e-6
NEG_INF = -1e30

NA_W = NA_HEADS * HEAD_DIM
DIL_W = DIL_HEADS_PER_GROUP * HEAD_DIM
MEM_W = MEM_HEADS * HEAD_DIM
LANES = 128
CHUNK = 256
ROW_TILE = 512
VMEM_LIMIT = 48 * 1024 * 1024
MIX_SUBTILES = 2
ROW_DMA_UNROLL = 8


def _cparams(n_axes):
    return pltpu.CompilerParams(dimension_semantics=("arbitrary",) * n_axes, vmem_limit_bytes=VMEM_LIMIT)


def _nt_dot(a, b):
    return lax.dot_general(a, b, (((1,), (1,)), ((), ())), preferred_element_type=F32)


def _pack_pair(a, b):
    ua = lax.bitcast_convert_type(a.astype(BF16).astype(F32), U32)
    ub = lax.bitcast_convert_type(b.astype(BF16).astype(F32), U32)
    return (ua >> 16) | (ub & jnp.uint32(0xFFFF0000))


def _unpack_pair(p):
    a = lax.bitcast_convert_type(p << 16, F32)
    b = lax.bitcast_convert_type(p & jnp.uint32(0xFFFF0000), F32)
    return a, b


def _proj_kernel(x_ref, g_ref, w_ref, gain_ref, bd_ref, *refs, plan, tm):
    n_buf = CHUNK // LANES
    out_refs, ybufs = refs[:-n_buf], refs[-n_buf:]
    x = x_ref[...]
    ms = jnp.mean(x * x, axis=-1, keepdims=True)
    h = (x * lax.rsqrt(ms + EPS) * g_ref[...]).astype(BF16)
    for c, (normed, oi, col, dil) in enumerate(plan):
        y = jnp.dot(h, w_ref[:, c * CHUNK:(c + 1) * CHUNK], preferred_element_type=F32)
        if normed:
            msq = jnp.dot((y * y).astype(BF16), bd_ref[...], preferred_element_type=F32)
            y = y * lax.rsqrt(msq + EPS) * gain_ref[c:c + 1, :]
        o_ref = out_refs[oi]
        if dil == 1:
            o_ref[:, col:col + CHUNK] = y.astype(o_ref.dtype)
        else:
            for j, ybuf in enumerate(ybufs):
                ybuf[...] = y[:, j * LANES:(j + 1) * LANES]
            for rho in range(dil):
                for j, ybuf in enumerate(ybufs):
                    rows = ybuf[pl.ds(rho, tm // dil, stride=dil), :]
                    o_ref[0, rho, :, col + j * LANES:col + (j + 1) * LANES] = rows.astype(o_ref.dtype)


def _project(x2d, g, w_bf16, gains, plan, out_shapes, out_specs, tm):
    n, d = x2d.shape
    cols = w_bf16.shape[1]
    bd = jnp.kron(jnp.eye(CHUNK // HEAD_DIM, dtype=F32), jnp.full((HEAD_DIM, HEAD_DIM), 1.0 / HEAD_DIM, F32)).astype(BF16)
    return pl.pallas_call(
        functools.partial(_proj_kernel, plan=plan, tm=tm),
        grid=(n // tm,),
        in_specs=[
            pl.BlockSpec((tm, d), lambda i: (i, 0)),
            pl.BlockSpec((1, d), lambda i: (0, 0)),
            pl.BlockSpec((d, cols), lambda i: (0, 0)),
            pl.BlockSpec(gains.shape, lambda i: (0, 0)),
            pl.BlockSpec((CHUNK, CHUNK), lambda i: (0, 0)),
        ],
        out_specs=out_specs,
        out_shape=out_shapes,
        scratch_shapes=[pltpu.VMEM((tm, LANES), F32)] * (CHUNK // LANES),
        compiler_params=_cparams(1),
        name="rmsnorm_project",
    )(x2d, g.reshape(1, d), w_bf16, gains, bd)


def _na_kernel(q_ref, k_ref, v_ref, bias_ref, o_ref, *, groups_per_step, n_rows):
    step = pl.program_id(2)
    n_groups = n_rows // NA_GROUP_ROWS
    tq = NA_GROUP_ROWS * GRID_W
    keys = NA_KEY_ROWS * GRID_W
    for i in range(groups_per_step):
        rg = step * groups_per_step + i
        key_row0 = jnp.clip(rg * NA_GROUP_ROWS - NA_WIN_ROWS // 2, 0, n_rows - NA_KEY_ROWS)
        gtype = jnp.where(rg == 0, 0, jnp.where(rg == n_groups - 1, 2, 1))
        k0 = pl.multiple_of(key_row0 * GRID_W, GRID_W)
        q = q_ref[0, i * tq:(i + 1) * tq, :]
        k = k_ref[0, pl.ds(k0, keys), :]
        v = v_ref[0, pl.ds(k0, keys), :]
        outs = []
        for hh in range(LANES // HEAD_DIM):
            sl = slice(hh * HEAD_DIM, (hh + 1) * HEAD_DIM)
            s = _nt_dot(q[:, sl], k[:, sl]) + bias_ref[hh, gtype]
            m = jnp.max(s, axis=-1, keepdims=True)
            e = jnp.exp(s - m)
            den = jnp.sum(e, axis=-1, keepdims=True)
            o = jnp.dot(e.astype(BF16), v[:, sl], preferred_element_type=F32)
            outs.append(o / den)
        o_ref[0, i * tq:(i + 1) * tq, :] = jnp.concatenate(outs, axis=-1).astype(o_ref.dtype)


def _toeplitz(g, rows, cols):
    p = g.shape[-1]
    tiled = jnp.tile(g, (1,) * (g.ndim - 1) + (rows,))[..., :rows * (p - 1)]
    return tiled.reshape(g.shape[:-1] + (rows, p - 1))[..., :cols]


def _na_bias_table(rpb):
    kh, kw = NA_WIN_ROWS, NA_WIN_COLS
    h = rpb.shape[0]
    rpb = rpb.astype(F32)
    by_row = jnp.stack([rpb[:, kh - 1 - t:2 * kh - 1 - t, :] for t in range(kh)], axis=1)
    lo = GRID_W - kw
    padded = jnp.pad(by_row, ((0, 0), (0, 0), (0, 0), (lo, 2 * GRID_W - lo - (2 * kw - 1))), constant_values=NEG_INF)
    vals = _toeplitz(jnp.roll(padded, -(GRID_W - 1), axis=-1), GRID_W, GRID_W)
    c = jnp.arange(GRID_W)[:, None]
    kc = jnp.arange(GRID_W)[None, :]
    c0 = jnp.clip(c - kw // 2, 0, GRID_W - kw)
    inside = (kc >= c0) & (kc < c0 + kw)
    per_row = jnp.where(inside, vals, NEG_INF).transpose(0, 1, 3, 2, 4)

    def placed(t, shift):
        return jnp.pad(per_row[:, t], ((0, 0), (0, 0), (shift, NA_KEY_ROWS - kh - shift), (0, 0)), constant_values=NEG_INF)

    mid = kh // 2
    top = jnp.stack([placed(i, 0) for i in range(NA_GROUP_ROWS)], axis=1)
    interior = jnp.stack([placed(mid, i) for i in range(NA_GROUP_ROWS)], axis=1)
    bottom = jnp.stack([placed(mid + i, NA_KEY_ROWS - kh) for i in range(NA_GROUP_ROWS)], axis=1)
    table = jnp.stack([top, interior, bottom], axis=1)
    return table.reshape(h, 3, NA_GROUP_ROWS * GRID_W, NA_KEY_ROWS * GRID_W)


def _neighbourhood_attention(pm3, bias_tab, b, s):
    n_rows = s // GRID_W
    groups_per_step = 2
    tq = groups_per_step * NA_GROUP_ROWS * GRID_W
    pairs = NA_W // LANES
    assert n_rows % (groups_per_step * NA_GROUP_ROWS) == 0 and n_rows >= NA_KEY_ROWS + NA_GROUP_ROWS
    return pl.pallas_call(
        functools.partial(_na_kernel, groups_per_step=groups_per_step, n_rows=n_rows),
        grid=(b, pairs, n_rows // (groups_per_step * NA_GROUP_ROWS)),
        in_specs=[
            pl.BlockSpec((1, tq, LANES), lambda bi, hp, rb: (bi, rb, hp)),
            pl.BlockSpec((1, s, LANES), lambda bi, hp, rb: (bi, 0, pairs + hp)),
            pl.BlockSpec((1, s, LANES), lambda bi, hp, rb: (bi, 0, 2 * pairs + hp)),
            pl.BlockSpec((LANES // HEAD_DIM, 3, NA_GROUP_ROWS * GRID_W, NA_KEY_ROWS * GRID_W), lambda bi, hp, rb: (hp, 0, 0, 0)),
        ],
        out_specs=pl.BlockSpec((1, tq, LANES), lambda bi, hp, rb: (bi, rb, hp)),
        out_shape=jax.ShapeDtypeStruct((b, s, NA_W), BF16),
        compiler_params=_cparams(3),
        name="neighbourhood_attention",
    )(pm3, pm3, pm3, bias_tab)


def _dil_kernel(q_ref, k_ref, v_ref, band_ref, o_ref, lse_ref, *, length):
    nblk = length // DIL_BLOCK
    span = DIL_BLOCK + 2 * DIL_SIDE

    def blk(i, carry):
        i0 = pl.multiple_of(i * DIL_BLOCK, DIL_BLOCK)
        p0 = pl.multiple_of(jnp.maximum(i0 - DIL_SIDE, 0), DIL_SIDE)
        n0 = pl.multiple_of(jnp.minimum(i0 + DIL_BLOCK, length - DIL_SIDE), DIL_SIDE)
        q = q_ref[0, 0, pl.ds(i0, DIL_BLOCK), :]
        k = jnp.concatenate([k_ref[0, 0, pl.ds(p0, DIL_SIDE), :], k_ref[0, 0, pl.ds(i0, DIL_BLOCK), :],
                             k_ref[0, 0, pl.ds(n0, DIL_SIDE), :]], axis=0)
        v = jnp.concatenate([v_ref[0, 0, pl.ds(p0, DIL_SIDE), :], v_ref[0, 0, pl.ds(i0, DIL_BLOCK), :],
                             v_ref[0, 0, pl.ds(n0, DIL_SIDE), :]], axis=0)
        kpos = i0 - DIL_SIDE + lax.broadcasted_iota(I32, (1, span), 1)
        valid = (kpos >= 0) & (kpos < length)
        outs, lses = [], []
        for hh in range(LANES // HEAD_DIM):
            sl = slice(hh * HEAD_DIM, (hh + 1) * HEAD_DIM)
            s = _nt_dot(q[:, sl], k[:, sl]) + band_ref[hh]
            s = jnp.where(valid, s, NEG_INF)
            m = jnp.max(s, axis=-1, keepdims=True)
            e = jnp.exp(s - m)
            den = jnp.sum(e, axis=-1, keepdims=True)
            o = jnp.dot(e.astype(BF16), v[:, sl], preferred_element_type=F32)
            outs.append(o / den)
            lses.append(jnp.broadcast_to(m + jnp.log(den), (DIL_BLOCK, HEAD_DIM)))
        o_ref[0, 0, pl.ds(i0, DIL_BLOCK), :] = jnp.concatenate(outs, axis=-1)
        lse_ref[0, 0, pl.ds(i0, DIL_BLOCK), :] = jnp.concatenate(lses, axis=-1)
        return carry

    unroll = math.gcd(nblk, DIL_UNROLL)

    def blk_group(j, carry):
        for u in range(unroll):
            blk(j * unroll + u, carry)
        return carry

    lax.fori_loop(0, nblk // unroll, blk_group, 0)


def _t5_bucket(rel):
    nb = T5_BUCKETS // 2
    max_exact = nb // 2
    n = jnp.abs(rel)
    upper = (rel > 0).astype(I32) * nb
    nf = jnp.maximum(n, 1).astype(F32)
    large = max_exact + (jnp.log(nf / max_exact) / math.log(T5_MAX_DIST / max_exact) * (nb - max_exact)).astype(I32)
    large = jnp.minimum(large, nb - 1)
    return upper + jnp.where(n < max_exact, n, large)


def _dil_band_table(t5, g, dil):
    offs = jnp.arange(-DIL_SIDE, DIL_SIDE + 1) * dil
    bias = t5[_t5_bucket(offs), g].T.astype(F32)
    span = DIL_BLOCK + 2 * DIL_SIDE
    period = DIL_BLOCK + span
    g_vec = jnp.pad(bias, ((0, 0), (0, period - bias.shape[1])), constant_values=NEG_INF)
    return _toeplitz(g_vec, DIL_BLOCK, span)


def _dilated_attention(dg, band, b, dil, length):
    pairs = DIL_W // LANES
    spec = lambda off: pl.BlockSpec((1, 1, length, LANES), lambda bi, rho, hp: (bi, rho, 0, off + hp))
    out_spec = pl.BlockSpec((1, 1, length, LANES), lambda bi, rho, hp: (bi, rho, 0, hp))
    shape = jax.ShapeDtypeStruct((b, dil, length, DIL_W), F32)
    return pl.pallas_call(
        functools.partial(_dil_kernel, length=length),
        grid=(b, dil, pairs),
        in_specs=[spec(0), spec(pairs), spec(2 * pairs),
                  pl.BlockSpec((LANES // HEAD_DIM, DIL_BLOCK, DIL_BLOCK + 2 * DIL_SIDE), lambda bi, rho, hp: (hp, 0, 0))],
        out_specs=(out_spec, out_spec),
        out_shape=(shape, shape),
        compiler_params=_cparams(3),
        name=f"dilated_attention_d{dil}",
    )(dg, dg, dg, band)


def _mix_kernel(x_ref, na_ref, qm_ref, o0_ref, l0_ref, o1_ref, l1_ref, o2_ref, l2_ref, kvm_ref, wout_ref, gffn_ref,
                wrh_ref, wrl_ref, br_ref, x1_ref, hp_ref, route_ref, *ibufs, tm, dils):
    sub = tm // MIX_SUBTILES
    bufs = iter(ibufs)
    for si in range(MIX_SUBTILES):
        _mix_rows(si, sub, bufs, x_ref, na_ref, qm_ref, (o0_ref, o1_ref, o2_ref), (l0_ref, l1_ref, l2_ref), kvm_ref,
                  wout_ref, gffn_ref, wrh_ref, wrl_ref, br_ref, x1_ref, hp_ref, route_ref, dils)


def _mix_rows(si, sub, bufs, x_ref, na_ref, qm_ref, o_refs, l_refs, kvm_ref, wout_ref, gffn_ref, wrh_ref, wrl_ref,
              br_ref, x1_ref, hp_ref, route_ref, dils):
    rows = slice(si * sub, (si + 1) * sub)

    def token_major(ref, dil):
        if dil == 1:
            return ref[0, 0, rows, :]
        n_cls = sub // dil
        mine = [next(bufs) for _ in range(DIL_W // LANES)]
        for rho in range(dil):
            for j, ibuf in enumerate(mine):
                ibuf[pl.ds(rho, n_cls, stride=dil), :] = ref[0, rho, si * n_cls:(si + 1) * n_cls, j * LANES:(j + 1) * LANES]
        return jnp.concatenate([ibuf[...] for ibuf in mine], axis=-1)

    lses = [token_major(l_ref, dil) for l_ref, dil in zip(l_refs, dils)]
    lmax = jnp.maximum(jnp.maximum(lses[0], lses[1]), lses[2])
    wts = [jnp.exp(l - lmax) for l in lses]
    num = None
    for o_ref, dil, w in zip(o_refs, dils, wts):
        term = w * token_major(o_ref, dil)
        num = term if num is None else num + term
    out_dil = num / (wts[0] + wts[1] + wts[2])

    qm = qm_ref[rows, :]
    mem_outs = []
    for hh in range(MEM_HEADS):
        sl = slice(hh * HEAD_DIM, (hh + 1) * HEAD_DIM)
        s = _nt_dot(qm[:, sl], kvm_ref[0, :, sl])
        m = jnp.max(s, axis=-1, keepdims=True)
        e = jnp.exp(s - m)
        den = jnp.sum(e, axis=-1, keepdims=True)
        vm = kvm_ref[0, :, MEM_W + hh * HEAD_DIM:MEM_W + (hh + 1) * HEAD_DIM]
        mem_outs.append(jnp.dot(e.astype(BF16), vm, preferred_element_type=F32) / den)
    out_mem = jnp.concatenate(mem_outs, axis=-1)

    y = jnp.dot(na_ref[rows, :], wout_ref[0:NA_W, :], preferred_element_type=F32)
    y = y + jnp.dot(out_dil.astype(BF16), wout_ref[NA_W:NA_W + DIL_W, :], preferred_element_type=F32)
    y = y + jnp.dot(out_mem.astype(BF16), wout_ref[NA_W + DIL_W:, :], preferred_element_type=F32)
    x1 = x_ref[rows, :] + y
    x1_ref[rows, :] = x1

    ms = jnp.mean(x1 * x1, axis=-1, keepdims=True)
    h = x1 * lax.rsqrt(ms + EPS) * gffn_ref[...]
    half = h.shape[1] // 2
    hp_ref[rows, :] = _pack_pair(h[:, :half], h[:, half:])
    h_hi = h.astype(BF16)
    h_lo = (h - h_hi.astype(F32)).astype(BF16)
    logits = (jnp.dot(h_hi, wrh_ref[...], preferred_element_type=F32)
              + (jnp.dot(h_hi, wrl_ref[...], preferred_element_type=F32)
                 + jnp.dot(h_lo, wrh_ref[...], preferred_element_type=F32))) + br_ref[...]
    lane = lax.broadcasted_iota(I32, logits.shape, 1)
    neg = -jnp.inf
    gl = jnp.where(lane < N_GROUPS, logits, neg)
    gmax = jnp.max(gl, axis=-1, keepdims=True)
    gidx = jnp.min(jnp.where(gl == gmax, lane, LANES), axis=-1, keepdims=True)
    grp_gate = 1.0 / jnp.sum(jnp.where(lane < N_GROUPS, jnp.exp(logits - gmax), 0.0), axis=-1, keepdims=True)
    lo = N_GROUPS + EXPERTS_PER_GROUP * gidx
    fl = jnp.where((lane >= lo) & (lane < lo + EXPERTS_PER_GROUP), logits, neg)
    v1 = jnp.max(fl, axis=-1, keepdims=True)
    i1 = jnp.min(jnp.where(fl == v1, lane, LANES), axis=-1, keepdims=True)
    fl2 = jnp.where(lane == i1, neg, fl)
    v2 = jnp.max(fl2, axis=-1, keepdims=True)
    i2 = jnp.min(jnp.where(fl2 == v2, lane, LANES), axis=-1, keepdims=True)
    t = jnp.exp(v2 - v1)
    g1 = grp_gate / (1.0 + t)
    g2 = grp_gate * t / (1.0 + t)
    e1 = (i1 - N_GROUPS).astype(F32)
    e2 = (i2 - N_GROUPS).astype(F32)
    route_ref[rows, :] = jnp.where(lane == 0, e1, jnp.where(lane == 1, e2, jnp.where(lane == 2, g1, jnp.where(lane == 3, g2, 0.0))))


def _mix(x2d, na, pm, qm_col, dil_outs, kvm, w_out_bf16, g_ffn, w_r, b_r, s):
    n, d = x2d.shape
    tm = ROW_TILE
    tpb = s // tm
    dils = tuple(dil for _, dil in DIL_PAIRS)
    w_r_hi = w_r.astype(BF16)
    w_r_lo = (w_r - w_r_hi.astype(F32)).astype(BF16)
    dil_specs, dil_args = [], []
    for (o, lse), dil in zip(dil_outs, dils):
        spec = pl.BlockSpec((1, dil, tm // dil, DIL_W), lambda i: (i // tpb, 0, i % tpb, 0))
        dil_specs += [spec, spec]
        dil_args += [o, lse]
    return pl.pallas_call(
        functools.partial(_mix_kernel, tm=tm, dils=dils),
        grid=(n // tm,),
        in_specs=[
            pl.BlockSpec((tm, d), lambda i: (i, 0)),
            pl.BlockSpec((tm, NA_W), lambda i: (i, 0)),
            pl.BlockSpec((tm, MEM_W), lambda i: (i, qm_col // MEM_W)),
            *dil_specs,
            pl.BlockSpec((1, kvm.shape[1], kvm.shape[2]), lambda i: (i // tpb, 0, 0)),
            pl.BlockSpec(w_out_bf16.shape, lambda i: (0, 0)),
            pl.BlockSpec((1, d), lambda i: (0, 0)),
            pl.BlockSpec(w_r.shape, lambda i: (0, 0)),
            pl.BlockSpec(w_r.shape, lambda i: (0, 0)),
            pl.BlockSpec(b_r.shape, lambda i: (0, 0)),
        ],
        out_specs=(
            pl.BlockSpec((tm, d), lambda i: (i, 0)),
            pl.BlockSpec((tm, d // 2), lambda i: (i, 0)),
            pl.BlockSpec((tm, LANES), lambda i: (i, 0)),
        ),
        out_shape=(
            jax.ShapeDtypeStruct((n, d), F32),
            jax.ShapeDtypeStruct((n, d // 2), U32),
            jax.ShapeDtypeStruct((n, LANES), F32),
        ),
        scratch_shapes=[pltpu.VMEM((tm // MIX_SUBTILES, LANES), F32)]
        * (MIX_SUBTILES * 2 * sum(dil > 1 for dil in dils) * (DIL_W // LANES)),
        compiler_params=_cparams(1),
        name="merge_memattn_outproj_router",
    )(x2d, na, pm, *dil_args, kvm, w_out_bf16, g_ffn.reshape(1, d), w_r_hi, w_r_lo, b_r)


def _dispatch_kernel(dest_hbm, h_ref, xs_init_hbm, xs_hbm, idx_smem, idx_sem, row_sem, *, tm):
    del xs_init_hbm
    i = pl.program_id(0)
    n_steps = pl.num_programs(0)
    slot = i % 2

    def idx_copy(step, sl):
        return pltpu.make_async_copy(dest_hbm.at[step], idx_smem.at[sl], idx_sem.at[sl])

    def row_copy(t, d):
        return pltpu.make_async_copy(h_ref.at[pl.ds(t, 1)], xs_hbm.at[pl.ds(d, 1)], row_sem)

    @pl.when(i == 0)
    def _():
        idx_copy(0, 0).start()

    idx_copy(i, slot).wait()

    @pl.when(i + 1 < n_steps)
    def _():
        idx_copy(i + 1, 1 - slot).start()

    def start(t, carry):
        row_copy(t, idx_smem[slot, 2 * t]).start(priority=0)
        row_copy(t, idx_smem[slot, 2 * t + 1]).start(priority=1)
        return carry

    def wait(t, carry):
        row_copy(0, 0).wait()
        row_copy(0, 0).wait()
        return carry

    lax.fori_loop(0, tm, start, 0, unroll=ROW_DMA_UNROLL)
    lax.fori_loop(0, tm, wait, 0, unroll=ROW_DMA_UNROLL)


def _dispatch(dest2d, h_packed, cap, tm):
    n, w = h_packed.shape
    xs_init = jnp.zeros((cap, w), U32)
    return pl.pallas_call(
        functools.partial(_dispatch_kernel, tm=tm),
        grid=(n // tm,),
        in_specs=[
            pl.BlockSpec(memory_space=pl.ANY),
            pl.BlockSpec((tm, w), lambda i: (i, 0)),
            pl.BlockSpec(memory_space=pl.ANY),
        ],
        out_specs=pl.BlockSpec(memory_space=pl.ANY),
        out_shape=jax.ShapeDtypeStruct((cap, w), U32),
        scratch_shapes=[pltpu.SMEM((2, 2 * tm), I32), pltpu.SemaphoreType.DMA((2,)), pltpu.SemaphoreType.DMA],
        input_output_aliases={2: 0},
        compiler_params=_cparams(1),
        name="moe_dispatch",
    )(dest2d, h_packed, xs_init)


def _expert_kernel(blk_exp_ref, n_used_ref, xs_ref, w1_ref, w3_ref, w2_ref, yb_ref):
    del blk_exp_ref

    @pl.when(pl.program_id(0) < n_used_ref[0])
    def _():
        a, b = _unpack_pair(xs_ref[...])
        a, b = a.astype(BF16), b.astype(BF16)
        half = a.shape[1]
        h1 = (jnp.dot(a, w1_ref[0, :half, :], preferred_element_type=F32)
              + jnp.dot(b, w1_ref[0, half:, :], preferred_element_type=F32))
        h3 = (jnp.dot(a, w3_ref[0, :half, :], preferred_element_type=F32)
              + jnp.dot(b, w3_ref[0, half:, :], preferred_element_type=F32))
        act = (h1 * jax.nn.sigmoid(h1) * h3).astype(BF16)
        y = jnp.dot(act, w2_ref[0], preferred_element_type=F32)
        yb_ref[...] = _pack_pair(y[:, :half], y[:, half:])

    @pl.when(pl.program_id(0) >= n_used_ref[0])
    def _():
        yb_ref[...] = jnp.zeros(yb_ref.shape, yb_ref.dtype)


def _experts(blk_exp, n_used, xs, w1, w3, w2):
    cap, w = xs.shape
    n_blk = cap // MOE_BLOCK
    d, de = w1.shape[1], w1.shape[2]
    row = lambda i, be, nu: (jnp.minimum(i, nu[0] - 1), 0)
    wsel = lambda i, be, nu: (be[jnp.minimum(i, nu[0] - 1)], 0, 0)
    return pl.pallas_call(
        _expert_kernel,
        grid_spec=pltpu.PrefetchScalarGridSpec(
            num_scalar_prefetch=2,
            grid=(n_blk,),
            in_specs=[
                pl.BlockSpec((MOE_BLOCK, w), row),
                pl.BlockSpec((1, d, de), wsel),
                pl.BlockSpec((1, d, de), wsel),
                pl.BlockSpec((1, de, d), wsel),
            ],
            out_specs=pl.BlockSpec((MOE_BLOCK, w), lambda i, be, nu: (i, 0)),
        ),
        out_shape=jax.ShapeDtypeStruct((cap, w), U32),
        compiler_params=_cparams(1),
        name="moe_experts",
    )(blk_exp, n_used, xs, w1, w3, w2)


def _combine_kernel(dest_hbm, yb_hbm, x1_ref, route_ref, o_ref, idx_smem, ybuf, idx_sem, row_sem, *, tm):
    i = pl.program_id(0)
    n_steps = pl.num_programs(0)
    slot = i % 2

    def idx_copy(step, sl):
        return pltpu.make_async_copy(dest_hbm.at[step], idx_smem.at[sl], idx_sem.at[sl])

    def row_copy(sl, t, k, d):
        return pltpu.make_async_copy(yb_hbm.at[pl.ds(d, 1)], ybuf.at[sl, k, pl.ds(t, 1)], row_sem.at[sl])

    def issue(sl):
        def start(t, carry):
            row_copy(sl, t, 0, idx_smem[sl, 2 * t]).start(priority=0)
            row_copy(sl, t, 1, idx_smem[sl, 2 * t + 1]).start(priority=1)
            return carry

        lax.fori_loop(0, tm, start, 0, unroll=ROW_DMA_UNROLL)

    @pl.when(i == 0)
    def _():
        idx_copy(0, 0).start()
        idx_copy(0, 0).wait()
        issue(0)

        @pl.when(n_steps > 1)
        def _():
            idx_copy(1, 1).start()

    @pl.when(i + 1 < n_steps)
    def _():
        idx_copy(i + 1, 1 - slot).wait()
        issue(1 - slot)

    @pl.when(i + 2 < n_steps)
    def _():
        idx_copy(i + 2, slot).start()

    def wait(t, carry):
        row_copy(slot, 0, 0, 0).wait()
        row_copy(slot, 0, 1, 0).wait()
        return carry

    lax.fori_loop(0, tm, wait, 0, unroll=ROW_DMA_UNROLL)
    a0, b0 = _unpack_pair(ybuf[slot, 0])
    a1, b1 = _unpack_pair(ybuf[slot, 1])
    g0 = route_ref[:, 2:3]
    g1 = route_ref[:, 3:4]
    half = a0.shape[1]
    o_ref[:, :half] = x1_ref[:, :half] + (g0 * a0 + g1 * a1)
    o_ref[:, half:] = x1_ref[:, half:] + (g0 * b0 + g1 * b1)


def _combine(dest2d, yb, x1, route, tm):
    n, d = x1.shape
    w = yb.shape[1]
    return pl.pallas_call(
        functools.partial(_combine_kernel, tm=tm),
        grid=(n // tm,),
        in_specs=[
            pl.BlockSpec(memory_space=pl.ANY),
            pl.BlockSpec(memory_space=pl.ANY),
            pl.BlockSpec((tm, d), lambda i: (i, 0)),
            pl.BlockSpec((tm, LANES), lambda i: (i, 0)),
        ],
        out_specs=pl.BlockSpec((tm, d), lambda i: (i, 0)),
        out_shape=jax.ShapeDtypeStruct((n, d), F32),
        scratch_shapes=[pltpu.SMEM((2, 2 * tm), I32), pltpu.VMEM((2, 2, tm, w), U32),
                        pltpu.SemaphoreType.DMA((2,)), pltpu.SemaphoreType.DMA((2,))],
        compiler_params=_cparams(1),
        name="moe_combine",
    )(dest2d, yb, x1, route)


def _routing_plan(route, n):
    e_flat = route[:, :2].astype(I32).reshape(-1)
    blk = MOE_BLOCK
    onehot = (e_flat[:, None] == jnp.arange(N_EXPERTS, dtype=I32)[None, :]).astype(BF16).reshape(-1, blk, N_EXPERTS)
    tri = (jnp.arange(blk)[:, None] > jnp.arange(blk)[None, :]).astype(BF16)
    within = jnp.einsum('ij,bjk->bik', tri, onehot, preferred_element_type=F32)
    blk_tot = jnp.sum(onehot.astype(F32), axis=1)
    blk_base = jnp.cumsum(blk_tot, axis=0) - blk_tot
    rank = jnp.sum((within + blk_base[:, None, :]) * onehot.astype(F32), axis=-1).reshape(-1).astype(I32)
    counts = jnp.sum(blk_tot, axis=0).astype(I32)
    padded = (counts + MOE_BLOCK - 1) // MOE_BLOCK * MOE_BLOCK
    pad_end = jnp.cumsum(padded)
    pad_start = pad_end - padded
    dest = pad_start[e_flat] + rank
    cap = 2 * n + N_EXPERTS * MOE_BLOCK
    n_blk = cap // MOE_BLOCK
    blk_pos = jnp.arange(n_blk, dtype=I32) * MOE_BLOCK
    blk_exp = jnp.minimum(jnp.sum((pad_end[None, :] <= blk_pos[:, None]).astype(I32), axis=1), N_EXPERTS - 1)
    n_used = (pad_end[-1:] // MOE_BLOCK).astype(I32)
    return dest.astype(I32), blk_exp, n_used, cap


def _layer(x, mem, g_mix, w_in, qk_gain, na_rpb, t5, g_mem, w_mem_kv, w_out, g_ffn, w_r1, b_r1, w_r2, b_r2, w1, w3, w2):
    b, s, d = x.shape
    n = b * s
    tm = ROW_TILE
    scale = HEAD_DIM ** -0.5
    dils = tuple(dil for _, dil in DIL_PAIRS)

    o_qd, o_kd, o_vd, o_qm = 3 * NA_W, 3 * NA_W + 3 * DIL_W, 3 * NA_W + 6 * DIL_W, 3 * NA_W + 9 * DIL_W
    w_bf = w_in.astype(BF16)
    col_blocks = [w_bf[:, :3 * NA_W], w_bf[:, o_qm:o_qm + MEM_W]]
    for g in range(len(DIL_PAIRS)):
        for base in (o_qd, o_kd, o_vd):
            col_blocks.append(w_bf[:, base + g * DIL_W:base + (g + 1) * DIL_W])
    w_perm = jnp.concatenate(col_blocks, axis=1)
    tile = lambda v: jnp.tile(v.astype(F32), CHUNK // HEAD_DIM)
    ones = jnp.ones((CHUNK,), F32)
    qa_g, ka_g = tile(qk_gain[0, 0]) * scale, tile(qk_gain[0, 1])
    qd_g, kd_g = tile(qk_gain[1, 0]) * scale, tile(qk_gain[1, 1])
    qm_g, km_g = tile(qk_gain[2, 0]) * scale, tile(qk_gain[2, 1])
    main_w = 3 * NA_W + MEM_W
    plan = [(True, 0, 0, 1), (True, 0, CHUNK, 1), (True, 0, 2 * CHUNK, 1), (True, 0, 3 * CHUNK, 1),
            (False, 0, 4 * CHUNK, 1), (False, 0, 5 * CHUNK, 1), (True, 0, 6 * CHUNK, 1)]
    gains = [qa_g, qa_g, ka_g, ka_g, ones, ones, qm_g]
    for gi, dil in enumerate(dils):
        plan += [(True, 1 + gi, 0, dil), (True, 1 + gi, DIL_W, dil), (False, 1 + gi, 2 * DIL_W, dil)]
        gains += [qd_g, kd_g, ones]
    tpb = s // tm
    out_shapes = [jax.ShapeDtypeStruct((n, main_w), BF16)]
    out_specs = [pl.BlockSpec((tm, main_w), lambda i: (i, 0))]
    for dil in dils:
        if dil == 1:
            out_shapes.append(jax.ShapeDtypeStruct((n, 3 * DIL_W), BF16))
            out_specs.append(pl.BlockSpec((tm, 3 * DIL_W), lambda i: (i, 0)))
        else:
            out_shapes.append(jax.ShapeDtypeStruct((b, dil, s // dil, 3 * DIL_W), BF16))
            out_specs.append(pl.BlockSpec((1, dil, tm // dil, 3 * DIL_W), lambda i: (i // tpb, 0, i % tpb, 0)))
    pm, *dgs = _project(x.reshape(n, d), g_mix, w_perm, jnp.stack(gains), tuple(plan), out_shapes, out_specs, tm)

    m_rows = mem.shape[0] * mem.shape[1]
    tmm = min(tm, m_rows)
    kv_plan = ((True, 0, 0, 1), (False, 0, CHUNK, 1))
    kvm = _project(mem.reshape(m_rows, d), g_mem, w_mem_kv.astype(BF16), jnp.stack([km_g, ones]), kv_plan,
                   [jax.ShapeDtypeStruct((m_rows, 2 * MEM_W), BF16)],
                   [pl.BlockSpec((tmm, 2 * MEM_W), lambda i: (i, 0))], tmm)[0]
    kvm = kvm.reshape(mem.shape[0], mem.shape[1], 2 * MEM_W)

    out_na = _neighbourhood_attention(pm.reshape(b, s, main_w), _na_bias_table(na_rpb), b, s)

    t5g = t5.reshape(T5_BUCKETS, len(DIL_PAIRS), DIL_HEADS_PER_GROUP)
    dil_outs = []
    for gi, (dg, dil) in enumerate(zip(dgs, dils)):
        dg = dg.reshape(b, dil, s // dil, 3 * DIL_W)
        dil_outs.append(_dilated_attention(dg, _dil_band_table(t5g, gi, dil), b, dil, s // dil))

    w_r = jnp.zeros((d, LANES), F32).at[:, :N_GROUPS].set(w_r1.astype(F32)).at[:, N_GROUPS:N_GROUPS + N_EXPERTS].set(w_r2.astype(F32))
    b_r = jnp.zeros((1, LANES), F32).at[0, :N_GROUPS].set(b_r1.astype(F32)).at[0, N_GROUPS:N_GROUPS + N_EXPERTS].set(b_r2.astype(F32))
    x1, h_packed, route = _mix(x.reshape(n, d), out_na.reshape(n, NA_W), pm, 3 * NA_W, dil_outs, kvm,
                               w_out.astype(BF16), g_ffn, w_r, b_r, s)

    dest, blk_exp, n_used, cap = _routing_plan(route, n)
    tmd = 256
    dest2d = dest.reshape(n // tmd, 2 * tmd)
    xs = _dispatch(dest2d, h_packed, cap, tmd)
    yb = _experts(blk_exp, n_used, xs, w1.astype(BF16), w3.astype(BF16), w2.astype(BF16))
    out = _combine(dest2d, yb, x1, route, tmd)
    return out.reshape(b, s, d)


def kernel(x, mem, g_mix, w_in, qk_gain, na_rpb, t5_table, g_mem, w_mem_kv, w_out, g_ffn, w_r1, b_r1, w_r2, b_r2, w1, w3, w2):
    for l in range(g_mix.shape[0]):
        x = _layer(x, mem, g_mix[l], w_in[l], qk_gain[l], na_rpb[l], t5_table, g_mem[l], w_mem_kv[l], w_out[l],
                   g_ffn[l], w_r1[l], b_r1[l], w_r2[l], b_r2[l], w1[l], w3[l], w2[l])
    return x
```

```python
import functools
import math

import jax
import jax.numpy as jnp
from jax import lax
from jax.experimental import pallas as pl
from jax.experimental.pallas import tpu as pltpu

F32 = jnp.float32
BF16 = jnp.bfloat16
U32 = jnp.uint32
I32 = jnp.int32

HEAD_DIM = 64
GRID_W = 64
NA_HEADS = 8
NA_WIN_ROWS = 8
NA_WIN_COLS = 16
NA_GROUP_ROWS = 4
NA_KEY_ROWS = NA_WIN_ROWS + NA_GROUP_ROWS - 1
DIL_PAIRS = ((128, 1), (512, 4), (2048, 16))
DIL_HEADS_PER_GROUP = 4
DIL_SIDE = 64
DIL_BLOCK = 128
DIL_UNROLL = 4
MEM_HEADS = 4
T5_BUCKETS = 32
T5_MAX_DIST = 1024
N_GROUPS = 4
EXPERTS_PER_GROUP = 8
N_EXPERTS = N_GROUPS * EXPERTS_PER_GROUP
MOE_BLOCK = 512
RANK_BLOCK = 256
EPS = 1e-6
NEG_INF = -1e30

NA_W = NA_HEADS * HEAD_DIM
DIL_W = DIL_HEADS_PER_GROUP * HEAD_DIM
MEM_W = MEM_HEADS * HEAD_DIM
LANES = 128
CHUNK = 256
ROW_TILE = 512
VMEM_LIMIT = 48 * 1024 * 1024
MIX_SUBTILES = 2
ROW_DMA_UNROLL = 8


def _cparams(n_axes):
    return pltpu.CompilerParams(dimension_semantics=("arbitrary",) * n_axes, vmem_limit_bytes=VMEM_LIMIT)


def _nt_dot(a, b):
    return lax.dot_general(a, b, (((1,), (1,)), ((), ())), preferred_element_type=F32)


def _pack_pair(a, b):
    ua = lax.bitcast_convert_type(a.astype(BF16).astype(F32), U32)
    ub = lax.bitcast_convert_type(b.astype(BF16).astype(F32), U32)
    return (ua >> 16) | (ub & jnp.uint32(0xFFFF0000))


def _unpack_pair(p):
    a = lax.bitcast_convert_type(p << 16, F32)
    b = lax.bitcast_convert_type(p & jnp.uint32(0xFFFF0000), F32)
    return a, b


def _proj_kernel(x_ref, g_ref, w_ref, gain_ref, bd_ref, *refs, plan, tm):
    n_buf = CHUNK // LANES
    out_refs, ybufs = refs[:-n_buf], refs[-n_buf:]
    x = x_ref[...]
    ms = jnp.mean(x * x, axis=-1, keepdims=True)
    h = (x * lax.rsqrt(ms + EPS) * g_ref[...]).astype(BF16)
    for c, (normed, oi, col, dil) in enumerate(plan):
        y = jnp.dot(h, w_ref[:, c * CHUNK:(c + 1) * CHUNK], preferred_element_type=F32)
        if normed:
            msq = jnp.dot((y * y).astype(BF16), bd_ref[...], preferred_element_type=F32)
            y = y * lax.rsqrt(msq + EPS) * gain_ref[c:c + 1, :]
        o_ref = out_refs[oi]
        if dil == 1:
            o_ref[:, col:col + CHUNK] = y.astype(o_ref.dtype)
        else:
            for j, ybuf in enumerate(ybufs):
                ybuf[...] = y[:, j * LANES:(j + 1) * LANES]
            for rho in range(dil):
                for j, ybuf in enumerate(ybufs):
                    rows = ybuf[pl.ds(rho, tm // dil, stride=dil), :]
                    o_ref[0, rho, :, col + j * LANES:col + (j + 1) * LANES] = rows.astype(o_ref.dtype)


def _project(x2d, g, w_bf16, gains, plan, out_shapes, out_specs, tm):
    n, d = x2d.shape
    cols = w_bf16.shape[1]
    bd = jnp.kron(jnp.eye(CHUNK // HEAD_DIM, dtype=F32), jnp.full((HEAD_DIM, HEAD_DIM), 1.0 / HEAD_DIM, F32)).astype(BF16)
    return pl.pallas_call(
        functools.partial(_proj_kernel, plan=plan, tm=tm),
        grid=(n // tm,),
        in_specs=[
            pl.BlockSpec((tm, d), lambda i: (i, 0)),
            pl.BlockSpec((1, d), lambda i: (0, 0)),
            pl.BlockSpec((d, cols), lambda i: (0, 0)),
            pl.BlockSpec(gains.shape, lambda i: (0, 0)),
            pl.BlockSpec((CHUNK, CHUNK), lambda i: (0, 0)),
        ],
        out_specs=out_specs,
        out_shape=out_shapes,
        scratch_shapes=[pltpu.VMEM((tm, LANES), F32)] * (CHUNK // LANES),
        compiler_params=_cparams(1),
        name="rmsnorm_project",
    )(x2d, g.reshape(1, d), w_bf16, gains, bd)


def _na_kernel(q_ref, k_ref, v_ref, bias_ref, o_ref, *, groups_per_step, n_rows):
    step = pl.program_id(2)
    n_groups = n_rows // NA_GROUP_ROWS
    tq = NA_GROUP_ROWS * GRID_W
    keys = NA_KEY_ROWS * GRID_W
    for i in range(groups_per_step):
        rg = step * groups_per_step + i
        key_row0 = jnp.clip(rg * NA_GROUP_ROWS - NA_WIN_ROWS // 2, 0, n_rows - NA_KEY_ROWS)
        gtype = jnp.where(rg == 0, 0, jnp.where(rg == n_groups - 1, 2, 1))
        k0 = pl.multiple_of(key_row0 * GRID_W, GRID_W)
        q = q_ref[0, i * tq:(i + 1) * tq, :]
        k = k_ref[0, pl.ds(k0, keys), :]
        v = v_ref[0, pl.ds(k0, keys), :]
        outs = []
        for hh in range(LANES // HEAD_DIM):
            sl = slice(hh * HEAD_DIM, (hh + 1) * HEAD_DIM)
            s = _nt_dot(q[:, sl], k[:, sl]) + bias_ref[hh, gtype]
            m = jnp.max(s, axis=-1, keepdims=True)
            e = jnp.exp(s - m)
            den = jnp.sum(e, axis=-1, keepdims=True)
            o = jnp.dot(e.astype(BF16), v[:, sl], preferred_element_type=F32)
            outs.append(o / den)
        o_ref[0, i * tq:(i + 1) * tq, :] = jnp.concatenate(outs, axis=-1).astype(o_ref.dtype)


def _toeplitz(g, rows, cols):
    p = g.shape[-1]
    tiled = jnp.tile(g, (1,) * (g.ndim - 1) + (rows,))[..., :rows * (p - 1)]
    return tiled.reshape(g.shape[:-1] + (rows, p - 1))[..., :cols]


def _na_bias_table(rpb):
    kh, kw = NA_WIN_ROWS, NA_WIN_COLS
    h = rpb.shape[0]
    rpb = rpb.astype(F32)
    by_row = jnp.stack([rpb[:, kh - 1 - t:2 * kh - 1 - t, :] for t in range(kh)], axis=1)
    lo = GRID_W - kw
    padded = jnp.pad(by_row, ((0, 0), (0, 0), (0, 0), (lo, 2 * GRID_W - lo - (2 * kw - 1))), constant_values=NEG_INF)
    vals = _toeplitz(jnp.roll(padded, -(GRID_W - 1), axis=-1), GRID_W, GRID_W)
    c = jnp.arange(GRID_W)[:, None]
    kc = jnp.arange(GRID_W)[None, :]
    c0 = jnp.clip(c - kw // 2, 0, GRID_W - kw)
    inside = (kc >= c0) & (kc < c0 + kw)
    per_row = jnp.where(inside, vals, NEG_INF).transpose(0, 1, 3, 2, 4)

    def placed(t, shift):
        return jnp.pad(per_row[:, t], ((0, 0), (0, 0), (shift, NA_KEY_ROWS - kh - shift), (0, 0)), constant_values=NEG_INF)

    mid = kh // 2
    top = jnp.stack([placed(i, 0) for i in range(NA_GROUP_ROWS)], axis=1)
    interior = jnp.stack([placed(mid, i) for i in range(NA_GROUP_ROWS)], axis=1)
    bottom = jnp.stack([placed(mid + i, NA_KEY_ROWS - kh) for i in range(NA_GROUP_ROWS)], axis=1)
    table = jnp.stack([top, interior, bottom], axis=1)
    return table.reshape(h, 3, NA_GROUP_ROWS * GRID_W, NA_KEY_ROWS * GRID_W)


def _neighbourhood_attention(pm3, bias_tab, b, s):
    n_rows = s // GRID_W
    groups_per_step = 2
    tq = groups_per_step * NA_GROUP_ROWS * GRID_W
    pairs = NA_W // LANES
    assert n_rows % (groups_per_step * NA_GROUP_ROWS) == 0 and n_rows >= NA_KEY_ROWS + NA_GROUP_ROWS
    return pl.pallas_call(
        functools.partial(_na_kernel, groups_per_step=groups_per_step, n_rows=n_rows),
        grid=(b, pairs, n_rows // (groups_per_step * NA_GROUP_ROWS)),
        in_specs=[
            pl.BlockSpec((1, tq, LANES), lambda bi, hp, rb: (bi, rb, hp)),
            pl.BlockSpec((1, s, LANES), lambda bi, hp, rb: (bi, 0, pairs + hp)),
            pl.BlockSpec((1, s, LANES), lambda bi, hp, rb: (bi, 0, 2 * pairs + hp)),
            pl.BlockSpec((LANES // HEAD_DIM, 3, NA_GROUP_ROWS * GRID_W, NA_KEY_ROWS * GRID_W), lambda bi, hp, rb: (hp, 0, 0, 0)),
        ],
        out_specs=pl.BlockSpec((1, tq, LANES), lambda bi, hp, rb: (bi, rb, hp)),
        out_shape=jax.ShapeDtypeStruct((b, s, NA_W), BF16),
        compiler_params=_cparams(3),
        name="neighbourhood_attention",
    )(pm3, pm3, pm3, bias_tab)


def _dil_kernel(q_ref, k_ref, v_ref, band_ref, o_ref, lse_ref, *, length):
    nblk = length // DIL_BLOCK
    span = DIL_BLOCK + 2 * DIL_SIDE

    def blk(i, carry):
        i0 = pl.multiple_of(i * DIL_BLOCK, DIL_BLOCK)
        p0 = pl.multiple_of(jnp.maximum(i0 - DIL_SIDE, 0), DIL_SIDE)
        n0 = pl.multiple_of(jnp.minimum(i0 + DIL_BLOCK, length - DIL_SIDE), DIL_SIDE)
        q = q_ref[0, 0, pl.ds(i0, DIL_BLOCK), :]
        k = jnp.concatenate([k_ref[0, 0, pl.ds(p0, DIL_SIDE), :], k_ref[0, 0, pl.ds(i0, DIL_BLOCK), :],
                             k_ref[0, 0, pl.ds(n0, DIL_SIDE), :]], axis=0)
        v = jnp.concatenate([v_ref[0, 0, pl.ds(p0, DIL_SIDE), :], v_ref[0, 0, pl.ds(i0, DIL_BLOCK), :],
                             v_ref[0, 0, pl.ds(n0, DIL_SIDE), :]], axis=0)
        kpos = i0 - DIL_SIDE + lax.broadcasted_iota(I32, (1, span), 1)
        valid = (kpos >= 0) & (kpos < length)
        outs, lses = [], []
        for hh in range(LANES // HEAD_DIM):
            sl = slice(hh * HEAD_DIM, (hh + 1) * HEAD_DIM)
            s = _nt_dot(q[:, sl], k[:, sl]) + band_ref[hh]
            s = jnp.where(valid, s, NEG_INF)
            m = jnp.max(s, axis=-1, keepdims=True)
            e = jnp.exp(s - m)
            den = jnp.sum(e, axis=-1, keepdims=True)
            o = jnp.dot(e.astype(BF16), v[:, sl], preferred_element_type=F32)
            outs.append(o / den)
            lses.append(jnp.broadcast_to(m + jnp.log(den), (DIL_BLOCK, HEAD_DIM)))
        o_ref[0, 0, pl.ds(i0, DIL_BLOCK), :] = jnp.concatenate(outs, axis=-1)
        lse_ref[0, 0, pl.ds(i0, DIL_BLOCK), :] = jnp.concatenate(lses, axis=-1)
        return carry

    unroll = math.gcd(nblk, DIL_UNROLL)

    def blk_group(j, carry):
        for u in range(unroll):
            blk(j * unroll + u, carry)
        return carry

    lax.fori_loop(0, nblk // unroll, blk_group, 0)


def _t5_bucket(rel):
    nb = T5_BUCKETS // 2
    max_exact = nb // 2
    n = jnp.abs(rel)
    upper = (rel > 0).astype(I32) * nb
    nf = jnp.maximum(n, 1).astype(F32)
    large = max_exact + (jnp.log(nf / max_exact) / math.log(T5_MAX_DIST / max_exact) * (nb - max_exact)).astype(I32)
    large = jnp.minimum(large, nb - 1)
    return upper + jnp.where(n < max_exact, n, large)


def _dil_band_table(t5, g, dil):
    offs = jnp.arange(-DIL_SIDE, DIL_SIDE + 1) * dil
    bias = t5[_t5_bucket(offs), g].T.astype(F32)
    span = DIL_BLOCK + 2 * DIL_SIDE
    period = DIL_BLOCK + span
    g_vec = jnp.pad(bias, ((0, 0), (0, period - bias.shape[1])), constant_values=NEG_INF)
    return _toeplitz(g_vec, DIL_BLOCK, span)


def _dilated_attention(dg, band, b, dil, length):
    pairs = DIL_W // LANES
    spec = lambda off: pl.BlockSpec((1, 1, length, LANES), lambda bi, rho, hp: (bi, rho, 0, off + hp))
    out_spec = pl.BlockSpec((1, 1, length, LANES), lambda bi, rho, hp: (bi, rho, 0, hp))
    shape = jax.ShapeDtypeStruct((b, dil, length, DIL_W), F32)
    return pl.pallas_call(
        functools.partial(_dil_kernel, length=length),
        grid=(b, dil, pairs),
        in_specs=[spec(0), spec(pairs), spec(2 * pairs),
                  pl.BlockSpec((LANES // HEAD_DIM, DIL_BLOCK, DIL_BLOCK + 2 * DIL_SIDE), lambda bi, rho, hp: (hp, 0, 0))],
        out_specs=(out_spec, out_spec),
        out_shape=(shape, shape),
        compiler_params=_cparams(3),
        name=f"dilated_attention_d{dil}",
    )(dg, dg, dg, band)


def _mix_kernel(x_ref, na_ref, qm_ref, o0_ref, l0_ref, o1_ref, l1_ref, o2_ref, l2_ref, kvm_ref, wout_ref, gffn_ref,
                wrh_ref, wrl_ref, br_ref, x1_ref, hp_ref, route_ref, *ibufs, tm, dils):
    sub = tm // MIX_SUBTILES
    bufs = iter(ibufs)
    for si in range(MIX_SUBTILES):
        _mix_rows(si, sub, bufs, x_ref, na_ref, qm_ref, (o0_ref, o1_ref, o2_ref), (l0_ref, l1_ref, l2_ref), kvm_ref,
                  wout_ref, gffn_ref, wrh_ref, wrl_ref, br_ref, x1_ref, hp_ref, route_ref, dils)


def _mix_rows(si, sub, bufs, x_ref, na_ref, qm_ref, o_refs, l_refs, kvm_ref, wout_ref, gffn_ref, wrh_ref, wrl_ref,
              br_ref, x1_ref, hp_ref, route_ref, dils):
    rows = slice(si * sub, (si + 1) * sub)

    def token_major(ref, dil):
        if dil == 1:
            return ref[0, 0, rows, :]
        n_cls = sub // dil
        mine = [next(bufs) for _ in range(DIL_W // LANES)]
        for rho in range(dil):
            for j, ibuf in enumerate(mine):
                ibuf[pl.ds(rho, n_cls, stride=dil), :] = ref[0, rho, si * n_cls:(si + 1) * n_cls, j * LANES:(j + 1) * LANES]
        return jnp.concatenate([ibuf[...] for ibuf in mine], axis=-1)

    lses = [token_major(l_ref, dil) for l_ref, dil in zip(l_refs, dils)]
    lmax = jnp.maximum(jnp.maximum(lses[0], lses[1]), lses[2])
    wts = [jnp.exp(l - lmax) for l in lses]
    num = None
    for o_ref, dil, w in zip(o_refs, dils, wts):
        term = w * token_major(o_ref, dil)
        num = term if num is None else num + term
    out_dil = num / (wts[0] + wts[1] + wts[2])

    qm = qm_ref[rows, :]
    mem_outs = []
    for hh in range(MEM_HEADS):
        sl = slice(hh * HEAD_DIM, (hh + 1) * HEAD_DIM)
        s = _nt_dot(qm[:, sl], kvm_ref[0, :, sl])
        m = jnp.max(s, axis=-1, keepdims=True)
        e = jnp.exp(s - m)
        den = jnp.sum(e, axis=-1, keepdims=True)
        vm = kvm_ref[0, :, MEM_W + hh * HEAD_DIM:MEM_W + (hh + 1) * HEAD_DIM]
        mem_outs.append(jnp.dot(e.astype(BF16), vm, preferred_element_type=F32) / den)
    out_mem = jnp.concatenate(mem_outs, axis=-1)

    y = jnp.dot(na_ref[rows, :], wout_ref[0:NA_W, :], preferred_element_type=F32)
    y = y + jnp.dot(out_dil.astype(BF16), wout_ref[NA_W:NA_W + DIL_W, :], preferred_element_type=F32)
    y = y + jnp.dot(out_mem.astype(BF16), wout_ref[NA_W + DIL_W:, :], preferred_element_type=F32)
    x1 = x_ref[rows, :] + y
    x1_ref[rows, :] = x1

    ms = jnp.mean(x1 * x1, axis=-1, keepdims=True)
    h = x1 * lax.rsqrt(ms + EPS) * gffn_ref[...]
    half = h.shape[1] // 2
    hp_ref[rows, :] = _pack_pair(h[:, :half], h[:, half:])
    h_hi = h.astype(BF16)
    h_lo = (h - h_hi.astype(F32)).astype(BF16)
    logits = (jnp.dot(h_hi, wrh_ref[...], preferred_element_type=F32)
              + (jnp.dot(h_hi, wrl_ref[...], preferred_element_type=F32)
                 + jnp.dot(h_lo, wrh_ref[...], preferred_element_type=F32))) + br_ref[...]
    lane = lax.broadcasted_iota(I32, logits.shape, 1)
    neg = -jnp.inf
    gl = jnp.where(lane < N_GROUPS, logits, neg)
    gmax = jnp.max(gl, axis=-1, keepdims=True)
    gidx = jnp.min(jnp.where(gl == gmax, lane, LANES), axis=-1, keepdims=True)
    grp_gate = 1.0 / jnp.sum(jnp.where(lane < N_GROUPS, jnp.exp(logits - gmax), 0.0), axis=-1, keepdims=True)
    lo = N_GROUPS + EXPERTS_PER_GROUP * gidx
    fl = jnp.where((lane >= lo) & (lane < lo + EXPERTS_PER_GROUP), logits, neg)
    v1 = jnp.max(fl, axis=-1, keepdims=True)
    i1 = jnp.min(jnp.where(fl == v1, lane, LANES), axis=-1, keepdims=True)
    fl2 = jnp.where(lane == i1, neg, fl)
    v2 = jnp.max(fl2, axis=-1, keepdims=True)
    i2 = jnp.min(jnp.where(fl2 == v2, lane, LANES), axis=-1, keepdims=True)
    t = jnp.exp(v2 - v1)
    g1 = grp_gate / (1.0 + t)
    g2 = grp_gate * t / (1.0 + t)
    e1 = (i1 - N_GROUPS).astype(F32)
    e2 = (i2 - N_GROUPS).astype(F32)
    route_ref[rows, :] = jnp.where(lane == 0, e1, jnp.where(lane == 1, e2, jnp.where(lane == 2, g1, jnp.where(lane == 3, g2, 0.0))))


def _mix(x2d, na, pm, qm_col, dil_outs, kvm, w_out_bf16, g_ffn, w_r, b_r, s):
    n, d = x2d.shape
    tm = ROW_TILE
    tpb = s // tm
    dils = tuple(dil for _, dil in DIL_PAIRS)
    w_r_hi = w_r.astype(BF16)
    w_r_lo = (w_r - w_r_hi.astype(F32)).astype(BF16)
    dil_specs, dil_args = [], []
    for (o, lse), dil in zip(dil_outs, dils):
        spec = pl.BlockSpec((1, dil, tm // dil, DIL_W), lambda i: (i // tpb, 0, i % tpb, 0))
        dil_specs += [spec, spec]
        dil_args += [o, lse]
    return pl.pallas_call(
        functools.partial(_mix_kernel, tm=tm, dils=dils),
        grid=(n // tm,),
        in_specs=[
            pl.BlockSpec((tm, d), lambda i: (i, 0)),
            pl.BlockSpec((tm, NA_W), lambda i: (i, 0)),
            pl.BlockSpec((tm, MEM_W), lambda i: (i, qm_col // MEM_W)),
            *dil_specs,
            pl.BlockSpec((1, kvm.shape[1], kvm.shape[2]), lambda i: (i // tpb, 0, 0)),
            pl.BlockSpec(w_out_bf16.shape, lambda i: (0, 0)),
            pl.BlockSpec((1, d), lambda i: (0, 0)),
            pl.BlockSpec(w_r.shape, lambda i: (0, 0)),
            pl.BlockSpec(w_r.shape, lambda i: (0, 0)),
            pl.BlockSpec(b_r.shape, lambda i: (0, 0)),
        ],
        out_specs=(
            pl.BlockSpec((tm, d), lambda i: (i, 0)),
            pl.BlockSpec((tm, d // 2), lambda i: (i, 0)),
            pl.BlockSpec((tm, LANES), lambda i: (i, 0)),
        ),
        out_shape=(
            jax.ShapeDtypeStruct((n, d), F32),
            jax.ShapeDtypeStruct((n, d // 2), U32),
            jax.ShapeDtypeStruct((n, LANES), F32),
        ),
        scratch_shapes=[pltpu.VMEM((tm // MIX_SUBTILES, LANES), F32)]
        * (MIX_SUBTILES * 2 * sum(dil > 1 for dil in dils) * (DIL_W // LANES)),
        compiler_params=_cparams(1),
        name="merge_memattn_outproj_router",
    )(x2d, na, pm, *dil_args, kvm, w_out_bf16, g_ffn.reshape(1, d), w_r_hi, w_r_lo, b_r)


def _dispatch_kernel(dest_hbm, h_ref, xs_init_hbm, xs_hbm, idx_smem, idx_sem, row_sem, *, tm):
    del xs_init_hbm
    i = pl.program_id(0)
    n_steps = pl.num_programs(0)
    slot = i % 2

    def idx_copy(step, sl):
        return pltpu.make_async_copy(dest_hbm.at[step], idx_smem.at[sl], idx_sem.at[sl])

    def row_copy(t, d):
        return pltpu.make_async_copy(h_ref.at[pl.ds(t, 1)], xs_hbm.at[pl.ds(d, 1)], row_sem)

    @pl.when(i == 0)
    def _():
        idx_copy(0, 0).start()

    def step(sl):
        idx_copy(i, sl).wait()

        @pl.when(i + 1 < n_steps)
        def _():
            idx_copy(i + 1, 1 - sl).start()

        def start(t, carry):
            row_copy(t, idx_smem[sl, 2 * t]).start(priority=0)
            row_copy(t, idx_smem[sl, 2 * t + 1]).start(priority=1)
            return carry

        def wait(t, carry):
            row_copy(0, 0).wait()
            row_copy(0, 0).wait()
            return carry

        lax.fori_loop(0, tm, start, 0, unroll=ROW_DMA_UNROLL)
        lax.fori_loop(0, tm, wait, 0, unroll=ROW_DMA_UNROLL)

    for sl in range(2):
        pl.when(slot == sl)(functools.partial(step, sl))


def _dispatch(dest2d, h_packed, cap, tm):
    n, w = h_packed.shape
    xs_init = jnp.zeros((cap, w), U32)
    return pl.pallas_call(
        functools.partial(_dispatch_kernel, tm=tm),
        grid=(n // tm,),
        in_specs=[
            pl.BlockSpec(memory_space=pl.ANY),
            pl.BlockSpec((tm, w), lambda i: (i, 0)),
            pl.BlockSpec(memory_space=pl.ANY),
        ],
        out_specs=pl.BlockSpec(memory_space=pl.ANY),
        out_shape=jax.ShapeDtypeStruct((cap, w), U32),
        scratch_shapes=[pltpu.SMEM((2, 2 * tm), I32), pltpu.SemaphoreType.DMA((2,)), pltpu.SemaphoreType.DMA],
        input_output_aliases={2: 0},
        compiler_params=_cparams(1),
        name="moe_dispatch",
    )(dest2d, h_packed, xs_init)


def _expert_kernel(blk_exp_ref, n_used_ref, xs_ref, w1_ref, w3_ref, w2_ref, yb_ref):
    del blk_exp_ref

    @pl.when(pl.program_id(0) < n_used_ref[0])
    def _():
        a, b = _unpack_pair(xs_ref[...])
        a, b = a.astype(BF16), b.astype(BF16)
        half = a.shape[1]
        h1 = (jnp.dot(a, w1_ref[0, :half, :], preferred_element_type=F32)
              + jnp.dot(b, w1_ref[0, half:, :], preferred_element_type=F32))
        h3 = (jnp.dot(a, w3_ref[0, :half, :], preferred_element_type=F32)
              + jnp.dot(b, w3_ref[0, half:, :], preferred_element_type=F32))
        act = (h1 * jax.nn.sigmoid(h1) * h3).astype(BF16)
        y = jnp.dot(act, w2_ref[0], preferred_element_type=F32)
        yb_ref[...] = _pack_pair(y[:, :half], y[:, half:])

    @pl.when(pl.program_id(0) >= n_used_ref[0])
    def _():
        yb_ref[...] = jnp.zeros(yb_ref.shape, yb_ref.dtype)


def _experts(blk_exp, n_used, xs, w1, w3, w2):
    cap, w = xs.shape
    n_blk = cap // MOE_BLOCK
    d, de = w1.shape[1], w1.shape[2]
    row = lambda i, be, nu: (jnp.minimum(i, nu[0] - 1), 0)
    wsel = lambda i, be, nu: (be[jnp.minimum(i, nu[0] - 1)], 0, 0)
    return pl.pallas_call(
        _expert_kernel,
        grid_spec=pltpu.PrefetchScalarGridSpec(
            num_scalar_prefetch=2,
            grid=(n_blk,),
            in_specs=[
                pl.BlockSpec((MOE_BLOCK, w), row),
                pl.BlockSpec((1, d, de), wsel),
                pl.BlockSpec((1, d, de), wsel),
                pl.BlockSpec((1, de, d), wsel),
            ],
            out_specs=pl.BlockSpec((MOE_BLOCK, w), lambda i, be, nu: (i, 0)),
        ),
        out_shape=jax.ShapeDtypeStruct((cap, w), U32),
        compiler_params=_cparams(1),
        name="moe_experts",
    )(blk_exp, n_used, xs, w1, w3, w2)


def _combine_kernel(dest_hbm, yb_hbm, x1_ref, route_ref, o_ref, idx_smem, ybuf, idx_sem, row_sem, *, tm):
    i = pl.program_id(0)
    n_steps = pl.num_programs(0)
    slot = i % 2

    def idx_copy(step, sl):
        return pltpu.make_async_copy(dest_hbm.at[step], idx_smem.at[sl], idx_sem.at[sl])

    def row_copy(sl, t, k, d):
        return pltpu.make_async_copy(yb_hbm.at[pl.ds(d, 1)], ybuf.at[sl, k, pl.ds(t, 1)], row_sem.at[sl])

    def issue(sl):
        def start(t, carry):
            row_copy(sl, t, 0, idx_smem[sl, 2 * t]).start(priority=0)
            row_copy(sl, t, 1, idx_smem[sl, 2 * t + 1]).start(priority=1)
            return carry

        lax.fori_loop(0, tm, start, 0, unroll=ROW_DMA_UNROLL)

    @pl.when(i == 0)
    def _():
        idx_copy(0, 0).start()
        idx_copy(0, 0).wait()
        issue(0)

        @pl.when(n_steps > 1)
        def _():
            idx_copy(1, 1).start()

    def step(sl):
        @pl.when(i + 1 < n_steps)
        def _():
            idx_copy(i + 1, 1 - sl).wait()
            issue(1 - sl)

        @pl.when(i + 2 < n_steps)
        def _():
            idx_copy(i + 2, sl).start()

        def wait(t, carry):
            row_copy(sl, 0, 0, 0).wait()
            row_copy(sl, 0, 1, 0).wait()
            return carry

        lax.fori_loop(0, tm, wait, 0, unroll=ROW_DMA_UNROLL)
        a0, b0 = _unpack_pair(ybuf[sl, 0])
        a1, b1 = _unpack_pair(ybuf[sl, 1])
        g0 = route_ref[:, 2:3]
        g1 = route_ref[:, 3:4]
        half = a0.shape[1]
        o_ref[:, :half] = x1_ref[:, :half] + (g0 * a0 + g1 * a1)
        o_ref[:, half:] = x1_ref[:, half:] + (g0 * b0 + g1 * b1)

    for sl in range(2):
        pl.when(slot == sl)(functools.partial(step, sl))


def _combine(dest2d, yb, x1, route, tm):
    n, d = x1.shape
    w = yb.shape[1]
    return pl.pallas_call(
        functools.partial(_combine_kernel, tm=tm),
        grid=(n // tm,),
        in_specs=[
            pl.BlockSpec(memory_space=pl.ANY),
            pl.BlockSpec(memory_space=pl.ANY),
            pl.BlockSpec((tm, d), lambda i: (i, 0)),
            pl.BlockSpec((tm, LANES), lambda i: (i, 0)),
        ],
        out_specs=pl.BlockSpec((tm, d), lambda i: (i, 0)),
        out_shape=jax.ShapeDtypeStruct((n, d), F32),
        scratch_shapes=[pltpu.SMEM((2, 2 * tm), I32), pltpu.VMEM((2, 2, tm, w), U32),
                        pltpu.SemaphoreType.DMA((2,)), pltpu.SemaphoreType.DMA((2,))],
        compiler_params=_cparams(1),
        name="moe_combine",
    )(dest2d, yb, x1, route)


def _routing_plan(route, n):
    e_flat = route[:, :2].astype(I32).reshape(-1)
    blk = RANK_BLOCK
    onehot = (e_flat[:, None] == jnp.arange(N_EXPERTS, dtype=I32)[None, :]).astype(BF16).reshape(-1, blk, N_EXPERTS)
    tri = (jnp.arange(blk)[:, None] > jnp.arange(blk)[None, :]).astype(BF16)
    within = jnp.einsum('ij,bjk->bik', tri, onehot, preferred_element_type=F32)
    blk_tot = jnp.sum(onehot.astype(F32), axis=1)
    blk_base = jnp.cumsum(blk_tot, axis=0) - blk_tot
    rank = jnp.sum((within + blk_base[:, None, :]) * onehot.astype(F32), axis=-1).reshape(-1).astype(I32)
    counts = jnp.sum(blk_tot, axis=0).astype(I32)
    padded = (counts + MOE_BLOCK - 1) // MOE_BLOCK * MOE_BLOCK
    pad_end = jnp.cumsum(padded)
    pad_start = pad_end - padded
    dest = pad_start[e_flat] + rank
    cap = 2 * n + N_EXPERTS * MOE_BLOCK
    n_blk = cap // MOE_BLOCK
    blk_pos = jnp.arange(n_blk, dtype=I32) * MOE_BLOCK
    blk_exp = jnp.minimum(jnp.sum((pad_end[None, :] <= blk_pos[:, None]).astype(I32), axis=1), N_EXPERTS - 1)
    n_used = (pad_end[-1:] // MOE_BLOCK).astype(I32)
    return dest.astype(I32), blk_exp, n_used, cap


def _layer(x, mem, g_mix, w_in, qk_gain, na_rpb, t5, g_mem, w_mem_kv, w_out, g_ffn, w_r1, b_r1, w_r2, b_r2, w1, w3, w2):
    b, s, d = x.shape
    n = b * s
    tm = ROW_TILE
    scale = HEAD_DIM ** -0.5
    dils = tuple(dil for _, dil in DIL_PAIRS)

    o_qd, o_kd, o_vd, o_qm = 3 * NA_W, 3 * NA_W + 3 * DIL_W, 3 * NA_W + 6 * DIL_W, 3 * NA_W + 9 * DIL_W
    w_bf = w_in.astype(BF16)
    col_blocks = [w_bf[:, :3 * NA_W], w_bf[:, o_qm:o_qm + MEM_W]]
    for g in range(len(DIL_PAIRS)):
        for base in (o_qd, o_kd, o_vd):
            col_blocks.append(w_bf[:, base + g * DIL_W:base + (g + 1) * DIL_W])
    w_perm = jnp.concatenate(col_blocks, axis=1)
    tile = lambda v: jnp.tile(v.astype(F32), CHUNK // HEAD_DIM)
    ones = jnp.ones((CHUNK,), F32)
    qa_g, ka_g = tile(qk_gain[0, 0]) * scale, tile(qk_gain[0, 1])
    qd_g, kd_g = tile(qk_gain[1, 0]) * scale, tile(qk_gain[1, 1])
    qm_g, km_g = tile(qk_gain[2, 0]) * scale, tile(qk_gain[2, 1])
    main_w = 3 * NA_W + MEM_W
    plan = [(True, 0, 0, 1), (True, 0, CHUNK, 1), (True, 0, 2 * CHUNK, 1), (True, 0, 3 * CHUNK, 1),
            (False, 0, 4 * CHUNK, 1), (False, 0, 5 * CHUNK, 1), (True, 0, 6 * CHUNK, 1)]
    gains = [qa_g, qa_g, ka_g, ka_g, ones, ones, qm_g]
    for gi, dil in enumerate(dils):
        plan += [(True, 1 + gi, 0, dil), (True, 1 + gi, DIL_W, dil), (False, 1 + gi, 2 * DIL_W, dil)]
        gains += [qd_g, kd_g, ones]
    tpb = s // tm
    out_shapes = [jax.ShapeDtypeStruct((n, main_w), BF16)]
    out_specs = [pl.BlockSpec((tm, main_w), lambda i: (i, 0))]
    for dil in dils:
        if dil == 1:
            out_shapes.append(jax.ShapeDtypeStruct((n, 3 * DIL_W), BF16))
            out_specs.append(pl.BlockSpec((tm, 3 * DIL_W), lambda i: (i, 0)))
        else:
            out_shapes.append(jax.ShapeDtypeStruct((b, dil, s // dil, 3 * DIL_W), BF16))
            out_specs.append(pl.BlockSpec((1, dil, tm // dil, 3 * DIL_W), lambda i: (i // tpb, 0, i % tpb, 0)))
    pm, *dgs = _project(x.reshape(n, d), g_mix, w_perm, jnp.stack(gains), tuple(plan), out_shapes, out_specs, tm)

    m_rows = mem.shape[0] * mem.shape[1]
    tmm = min(tm, m_rows)
    kv_plan = ((True, 0, 0, 1), (False, 0, CHUNK, 1))
    kvm = _project(mem.reshape(m_rows, d), g_mem, w_mem_kv.astype(BF16), jnp.stack([km_g, ones]), kv_plan,
                   [jax.ShapeDtypeStruct((m_rows, 2 * MEM_W), BF16)],
                   [pl.BlockSpec((tmm, 2 * MEM_W), lambda i: (i, 0))], tmm)[0]
    kvm = kvm.reshape(mem.shape[0], mem.shape[1], 2 * MEM_W)

    out_na = _neighbourhood_attention(pm.reshape(b, s, main_w), _na_bias_table(na_rpb), b, s)

    t5g = t5.reshape(T5_BUCKETS, len(DIL_PAIRS), DIL_HEADS_PER_GROUP)
    dil_outs = []
    for gi, (dg, dil) in enumerate(zip(dgs, dils)):
        dg = dg.reshape(b, dil, s // dil, 3 * DIL_W)
        dil_outs.append(_dilated_attention(dg, _dil_band_table(t5g, gi, dil), b, dil, s // dil))

    w_r = jnp.zeros((d, LANES), F32).at[:, :N_GROUPS].set(w_r1.astype(F32)).at[:, N_GROUPS:N_GROUPS + N_EXPERTS].set(w_r2.astype(F32))
    b_r = jnp.zeros((1, LANES), F32).at[0, :N_GROUPS].set(b_r1.astype(F32)).at[0, N_GROUPS:N_GROUPS + N_EXPERTS].set(b_r2.astype(F32))
    x1, h_packed, route = _mix(x.reshape(n, d), out_na.reshape(n, NA_W), pm, 3 * NA_W, dil_outs, kvm,
                               w_out.astype(BF16), g_ffn, w_r, b_r, s)

    dest, blk_exp, n_used, cap = _routing_plan(route, n)
    tmd = 256
    dest2d = dest.reshape(n // tmd, 2 * tmd)
    xs = _dispatch(dest2d, h_packed, cap, tmd)
    yb = _experts(blk_exp, n_used, xs, w1.astype(BF16), w3.astype(BF16), w2.astype(BF16))
    out = _combine(dest2d, yb, x1, route, tmd)
    return out.reshape(b, s, d)


def kernel(x, mem, g_mix, w_in, qk_gain, na_rpb, t5_table, g_mem, w_mem_kv, w_out, g_ffn, w_r1, b_r1, w_r2, b_r2, w1, w3, w2):
    for l in range(g_mix.shape[0]):
        x = _layer(x, mem, g_mix[l], w_in[l], qk_gain[l], na_rpb[l], t5_table, g_mem[l], w_mem_kv[l], w_out[l],
                   g_ffn[l], w_r1[l], b_r1[l], w_r2[l], b_r2[l], w1[l], w3[l], w2[l])
    return x
```

```python
import functools
import math

import jax
import jax.numpy as jnp
from jax import lax
from jax.experimental import pallas as pl
from jax.experimental.pallas import tpu as pltpu

F32 = jnp.float32
BF16 = jnp.bfloat16
U32 = jnp.uint32
I32 = jnp.int32

HEAD_DIM = 64
GRID_W = 64
NA_HEADS = 8
NA_WIN_ROWS = 8
NA_WIN_COLS = 16
NA_GROUP_ROWS = 4
NA_KEY_ROWS = NA_WIN_ROWS + NA_GROUP_ROWS - 1
DIL_PAIRS = ((128, 1), (512, 4), (2048, 16))
DIL_HEADS_PER_GROUP = 4
DIL_SIDE = 64
DIL_BLOCK = 128
DIL_UNROLL = 4
MEM_HEADS = 4
T5_BUCKETS = 32
T5_MAX_DIST = 1024
N_GROUPS = 4
EXPERTS_PER_GROUP = 8
N_EXPERTS = N_GROUPS * EXPERTS_PER_GROUP
MOE_BLOCK = 512
RANK_BLOCK = 256
EPS = 1e-6
NEG_INF = -1e30

NA_W = NA_HEADS * HEAD_DIM
DIL_W = DIL_HEADS_PER_GROUP * HEAD_DIM
MEM_W = MEM_HEADS * HEAD_DIM
LANES = 128
CHUNK = 256
ROW_TILE = 512
PROJ_TILE = 1024
VMEM_LIMIT = 56 * 1024 * 1024
MIX_SUBTILES = 2
ROW_DMA_UNROLL = 8


def _cparams(n_axes):
    return pltpu.CompilerParams(dimension_semantics=("arbitrary",) * n_axes, vmem_limit_bytes=VMEM_LIMIT)


def _nt_dot(a, b):
    return lax.dot_general(a, b, (((1,), (1,)), ((), ())), preferred_element_type=F32)


def _pack_pair(a, b):
    ua = lax.bitcast_convert_type(a.astype(BF16).astype(F32), U32)
    ub = lax.bitcast_convert_type(b.astype(BF16).astype(F32), U32)
    return (ua >> 16) | (ub & jnp.uint32(0xFFFF0000))


def _unpack_pair(p):
    a = lax.bitcast_convert_type(p << 16, F32)
    b = lax.bitcast_convert_type(p & jnp.uint32(0xFFFF0000), F32)
    return a, b


def _proj_kernel(x_ref, g_ref, w_ref, gain_ref, bd_ref, *refs, plan, tm):
    n_buf = CHUNK // LANES
    out_refs, ybufs = refs[:-n_buf], refs[-n_buf:]
    x = x_ref[...]
    ms = jnp.mean(x * x, axis=-1, keepdims=True)
    h = (x * lax.rsqrt(ms + EPS) * g_ref[...]).astype(BF16)
    for c, (normed, oi, col, dil) in enumerate(plan):
        y = jnp.dot(h, w_ref[:, c * CHUNK:(c + 1) * CHUNK], preferred_element_type=F32)
        if normed:
            msq = jnp.dot((y * y).astype(BF16), bd_ref[...], preferred_element_type=F32)
            y = y * lax.rsqrt(msq + EPS) * gain_ref[c:c + 1, :]
        o_ref = out_refs[oi]
        if dil == 1:
            o_ref[:, col:col + CHUNK] = y.astype(o_ref.dtype)
        else:
            for j, ybuf in enumerate(ybufs):
                ybuf[...] = y[:, j * LANES:(j + 1) * LANES]
            for rho in range(dil):
                for j, ybuf in enumerate(ybufs):
                    rows = ybuf[pl.ds(rho, tm // dil, stride=dil), :]
                    o_ref[0, rho, :, col + j * LANES:col + (j + 1) * LANES] = rows.astype(o_ref.dtype)


def _project(x2d, g, w_bf16, gains, plan, out_shapes, out_specs, tm):
    n, d = x2d.shape
    cols = w_bf16.shape[1]
    bd = jnp.kron(jnp.eye(CHUNK // HEAD_DIM, dtype=F32), jnp.full((HEAD_DIM, HEAD_DIM), 1.0 / HEAD_DIM, F32)).astype(BF16)
    return pl.pallas_call(
        functools.partial(_proj_kernel, plan=plan, tm=tm),
        grid=(n // tm,),
        in_specs=[
            pl.BlockSpec((tm, d), lambda i: (i, 0)),
            pl.BlockSpec((1, d), lambda i: (0, 0)),
            pl.BlockSpec((d, cols), lambda i: (0, 0)),
            pl.BlockSpec(gains.shape, lambda i: (0, 0)),
            pl.BlockSpec((CHUNK, CHUNK), lambda i: (0, 0)),
        ],
        out_specs=out_specs,
        out_shape=out_shapes,
        scratch_shapes=[pltpu.VMEM((tm, LANES), F32)] * (CHUNK // LANES),
        compiler_params=_cparams(1),
        name="rmsnorm_project",
    )(x2d, g.reshape(1, d), w_bf16, gains, bd)


def _na_kernel(q_ref, k_ref, v_ref, bias_ref, o_ref, *, groups_per_step, n_rows):
    step = pl.program_id(2)
    n_groups = n_rows // NA_GROUP_ROWS
    tq = NA_GROUP_ROWS * GRID_W
    keys = NA_KEY_ROWS * GRID_W
    for i in range(groups_per_step):
        rg = step * groups_per_step + i
        key_row0 = jnp.clip(rg * NA_GROUP_ROWS - NA_WIN_ROWS // 2, 0, n_rows - NA_KEY_ROWS)
        gtype = jnp.where(rg == 0, 0, jnp.where(rg == n_groups - 1, 2, 1))
        k0 = pl.multiple_of(key_row0 * GRID_W, GRID_W)
        q = q_ref[0, i * tq:(i + 1) * tq, :]
        k = k_ref[0, pl.ds(k0, keys), :]
        v = v_ref[0, pl.ds(k0, keys), :]
        outs = []
        for hh in range(LANES // HEAD_DIM):
            sl = slice(hh * HEAD_DIM, (hh + 1) * HEAD_DIM)
            s = _nt_dot(q[:, sl], k[:, sl]) + bias_ref[hh, gtype]
            m = jnp.max(s, axis=-1, keepdims=True)
            e = jnp.exp(s - m)
            den = jnp.sum(e, axis=-1, keepdims=True)
            o = jnp.dot(e.astype(BF16), v[:, sl], preferred_element_type=F32)
            outs.append(o / den)
        o_ref[0, i * tq:(i + 1) * tq, :] = jnp.concatenate(outs, axis=-1).astype(o_ref.dtype)


def _toeplitz(g, rows, cols):
    p = g.shape[-1]
    tiled = jnp.tile(g, (1,) * (g.ndim - 1) + (rows,))[..., :rows * (p - 1)]
    return tiled.reshape(g.shape[:-1] + (rows, p - 1))[..., :cols]


def _na_bias_table(rpb):
    kh, kw = NA_WIN_ROWS, NA_WIN_COLS
    h = rpb.shape[0]
    rpb = rpb.astype(F32)
    by_row = jnp.stack([rpb[:, kh - 1 - t:2 * kh - 1 - t, :] for t in range(kh)], axis=1)
    lo = GRID_W - kw
    padded = jnp.pad(by_row, ((0, 0), (0, 0), (0, 0), (lo, 2 * GRID_W - lo - (2 * kw - 1))), constant_values=NEG_INF)
    vals = _toeplitz(jnp.roll(padded, -(GRID_W - 1), axis=-1), GRID_W, GRID_W)
    c = jnp.arange(GRID_W)[:, None]
    kc = jnp.arange(GRID_W)[None, :]
    c0 = jnp.clip(c - kw // 2, 0, GRID_W - kw)
    inside = (kc >= c0) & (kc < c0 + kw)
    per_row = jnp.where(inside, vals, NEG_INF).transpose(0, 1, 3, 2, 4)

    def placed(t, shift):
        return jnp.pad(per_row[:, t], ((0, 0), (0, 0), (shift, NA_KEY_ROWS - kh - shift), (0, 0)), constant_values=NEG_INF)

    mid = kh // 2
    top = jnp.stack([placed(i, 0) for i in range(NA_GROUP_ROWS)], axis=1)
    interior = jnp.stack([placed(mid, i) for i in range(NA_GROUP_ROWS)], axis=1)
    bottom = jnp.stack([placed(mid + i, NA_KEY_ROWS - kh) for i in range(NA_GROUP_ROWS)], axis=1)
    table = jnp.stack([top, interior, bottom], axis=1)
    return table.reshape(h, 3, NA_GROUP_ROWS * GRID_W, NA_KEY_ROWS * GRID_W)


def _neighbourhood_attention(pm3, bias_tab, b, s):
    n_rows = s // GRID_W
    groups_per_step = 4
    tq = groups_per_step * NA_GROUP_ROWS * GRID_W
    pairs = NA_W // LANES
    assert n_rows % (groups_per_step * NA_GROUP_ROWS) == 0 and n_rows >= NA_KEY_ROWS + NA_GROUP_ROWS
    return pl.pallas_call(
        functools.partial(_na_kernel, groups_per_step=groups_per_step, n_rows=n_rows),
        grid=(b, pairs, n_rows // (groups_per_step * NA_GROUP_ROWS)),
        in_specs=[
            pl.BlockSpec((1, tq, LANES), lambda bi, hp, rb: (bi, rb, hp)),
            pl.BlockSpec((1, s, LANES), lambda bi, hp, rb: (bi, 0, pairs + hp)),
            pl.BlockSpec((1, s, LANES), lambda bi, hp, rb: (bi, 0, 2 * pairs + hp)),
            pl.BlockSpec((LANES // HEAD_DIM, 3, NA_GROUP_ROWS * GRID_W, NA_KEY_ROWS * GRID_W), lambda bi, hp, rb: (hp, 0, 0, 0)),
        ],
        out_specs=pl.BlockSpec((1, tq, LANES), lambda bi, hp, rb: (bi, rb, hp)),
        out_shape=jax.ShapeDtypeStruct((b, s, NA_W), BF16),
        compiler_params=_cparams(3),
        name="neighbourhood_attention",
    )(pm3, pm3, pm3, bias_tab)


def _dil_kernel(q_ref, k_ref, v_ref, band_ref, o_ref, lse_ref, *, length):
    nblk = length // DIL_BLOCK
    span = DIL_BLOCK + 2 * DIL_SIDE

    def blk(i, carry):
        i0 = pl.multiple_of(i * DIL_BLOCK, DIL_BLOCK)
        p0 = pl.multiple_of(jnp.maximum(i0 - DIL_SIDE, 0), DIL_SIDE)
        n0 = pl.multiple_of(jnp.minimum(i0 + DIL_BLOCK, length - DIL_SIDE), DIL_SIDE)
        q = q_ref[0, 0, pl.ds(i0, DIL_BLOCK), :]
        k = jnp.concatenate([k_ref[0, 0, pl.ds(p0, DIL_SIDE), :], k_ref[0, 0, pl.ds(i0, DIL_BLOCK), :],
                             k_ref[0, 0, pl.ds(n0, DIL_SIDE), :]], axis=0)
        v = jnp.concatenate([v_ref[0, 0, pl.ds(p0, DIL_SIDE), :], v_ref[0, 0, pl.ds(i0, DIL_BLOCK), :],
                             v_ref[0, 0, pl.ds(n0, DIL_SIDE), :]], axis=0)
        kpos = i0 - DIL_SIDE + lax.broadcasted_iota(I32, (1, span), 1)
        valid = (kpos >= 0) & (kpos < length)
        outs, lses = [], []
        for hh in range(LANES // HEAD_DIM):
            sl = slice(hh * HEAD_DIM, (hh + 1) * HEAD_DIM)
            s = _nt_dot(q[:, sl], k[:, sl]) + band_ref[hh]
            s = jnp.where(valid, s, NEG_INF)
            m = jnp.max(s, axis=-1, keepdims=True)
            e = jnp.exp(s - m)
            den = jnp.sum(e, axis=-1, keepdims=True)
            o = jnp.dot(e.astype(BF16), v[:, sl], preferred_element_type=F32)
            outs.append(o / den)
            lses.append(jnp.broadcast_to(m + jnp.log(den), (DIL_BLOCK, HEAD_DIM)))
        o_ref[0, 0, pl.ds(i0, DIL_BLOCK), :] = jnp.concatenate(outs, axis=-1)
        lse_ref[0, 0, pl.ds(i0, DIL_BLOCK), :] = jnp.concatenate(lses, axis=-1)
        return carry

    unroll = math.gcd(nblk, DIL_UNROLL)

    def blk_group(j, carry):
        for u in range(unroll):
            blk(j * unroll + u, carry)
        return carry

    lax.fori_loop(0, nblk // unroll, blk_group, 0)


def _t5_bucket(rel):
    nb = T5_BUCKETS // 2
    max_exact = nb // 2
    n = jnp.abs(rel)
    upper = (rel > 0).astype(I32) * nb
    nf = jnp.maximum(n, 1).astype(F32)
    large = max_exact + (jnp.log(nf / max_exact) / math.log(T5_MAX_DIST / max_exact) * (nb - max_exact)).astype(I32)
    large = jnp.minimum(large, nb - 1)
    return upper + jnp.where(n < max_exact, n, large)


def _dil_band_table(t5, g, dil):
    offs = jnp.arange(-DIL_SIDE, DIL_SIDE + 1) * dil
    bias = t5[_t5_bucket(offs), g].T.astype(F32)
    span = DIL_BLOCK + 2 * DIL_SIDE
    period = DIL_BLOCK + span
    g_vec = jnp.pad(bias, ((0, 0), (0, period - bias.shape[1])), constant_values=NEG_INF)
    return _toeplitz(g_vec, DIL_BLOCK, span)


def _dilated_attention(dg, band, b, dil, length):
    pairs = DIL_W // LANES
    spec = lambda off: pl.BlockSpec((1, 1, length, LANES), lambda bi, rho, hp: (bi, rho, 0, off + hp))
    out_spec = pl.BlockSpec((1, 1, length, LANES), lambda bi, rho, hp: (bi, rho, 0, hp))
    shape = jax.ShapeDtypeStruct((b, dil, length, DIL_W), F32)
    return pl.pallas_call(
        functools.partial(_dil_kernel, length=length),
        grid=(b, dil, pairs),
        in_specs=[spec(0), spec(pairs), spec(2 * pairs),
                  pl.BlockSpec((LANES // HEAD_DIM, DIL_BLOCK, DIL_BLOCK + 2 * DIL_SIDE), lambda bi, rho, hp: (hp, 0, 0))],
        out_specs=(out_spec, out_spec),
        out_shape=(shape, shape),
        compiler_params=_cparams(3),
        name=f"dilated_attention_d{dil}",
    )(dg, dg, dg, band)


def _mix_kernel(x_ref, na_ref, qm_ref, o0_ref, l0_ref, o1_ref, l1_ref, o2_ref, l2_ref, kvm_ref, wout_ref, gffn_ref,
                wrh_ref, wrl_ref, br_ref, x1_ref, hp_ref, route_ref, *ibufs, tm, dils):
    sub = tm // MIX_SUBTILES
    bufs = iter(ibufs)
    for si in range(MIX_SUBTILES):
        _mix_rows(si, sub, bufs, x_ref, na_ref, qm_ref, (o0_ref, o1_ref, o2_ref), (l0_ref, l1_ref, l2_ref), kvm_ref,
                  wout_ref, gffn_ref, wrh_ref, wrl_ref, br_ref, x1_ref, hp_ref, route_ref, dils)


def _mix_rows(si, sub, bufs, x_ref, na_ref, qm_ref, o_refs, l_refs, kvm_ref, wout_ref, gffn_ref, wrh_ref, wrl_ref,
              br_ref, x1_ref, hp_ref, route_ref, dils):
    rows = slice(si * sub, (si + 1) * sub)

    def token_major(ref, dil):
        if dil == 1:
            return ref[0, 0, rows, :]
        n_cls = sub // dil
        mine = [next(bufs) for _ in range(DIL_W // LANES)]
        for rho in range(dil):
            for j, ibuf in enumerate(mine):
                ibuf[pl.ds(rho, n_cls, stride=dil), :] = ref[0, rho, si * n_cls:(si + 1) * n_cls, j * LANES:(j + 1) * LANES]
        return jnp.concatenate([ibuf[...] for ibuf in mine], axis=-1)

    lses = [token_major(l_ref, dil) for l_ref, dil in zip(l_refs, dils)]
    lmax = jnp.maximum(jnp.maximum(lses[0], lses[1]), lses[2])
    wts = [jnp.exp(l - lmax) for l in lses]
    num = None
    for o_ref, dil, w in zip(o_refs, dils, wts):
        term = w * token_major(o_ref, dil)
        num = term if num is None else num + term
    out_dil = num / (wts[0] + wts[1] + wts[2])

    qm = qm_ref[rows, :]
    mem_outs = []
    for hh in range(MEM_HEADS):
        sl = slice(hh * HEAD_DIM, (hh + 1) * HEAD_DIM)
        s = _nt_dot(qm[:, sl], kvm_ref[0, :, sl])
        m = jnp.max(s, axis=-1, keepdims=True)
        e = jnp.exp(s - m)
        den = jnp.sum(e, axis=-1, keepdims=True)
        vm = kvm_ref[0, :, MEM_W + hh * HEAD_DIM:MEM_W + (hh + 1) * HEAD_DIM]
        mem_outs.append(jnp.dot(e.astype(BF16), vm, preferred_element_type=F32) / den)
    out_mem = jnp.concatenate(mem_outs, axis=-1)

    y = jnp.dot(na_ref[rows, :], wout_ref[0:NA_W, :], preferred_element_type=F32)
    y = y + jnp.dot(out_dil.astype(BF16), wout_ref[NA_W:NA_W + DIL_W, :], preferred_element_type=F32)
    y = y + jnp.dot(out_mem.astype(BF16), wout_ref[NA_W + DIL_W:, :], preferred_element_type=F32)
    x1 = x_ref[rows, :] + y
    x1_ref[rows, :] = x1

    ms = jnp.mean(x1 * x1, axis=-1, keepdims=True)
    h = x1 * lax.rsqrt(ms + EPS) * gffn_ref[...]
    half = h.shape[1] // 2
    hp_ref[rows, :] = _pack_pair(h[:, :half], h[:, half:])
    h_hi = h.astype(BF16)
    h_lo = (h - h_hi.astype(F32)).astype(BF16)
    logits = (jnp.dot(h_hi, wrh_ref[...], preferred_element_type=F32)
              + (jnp.dot(h_hi, wrl_ref[...], preferred_element_type=F32)
                 + jnp.dot(h_lo, wrh_ref[...], preferred_element_type=F32))) + br_ref[...]
    lane = lax.broadcasted_iota(I32, logits.shape, 1)
    neg = -jnp.inf
    gl = jnp.where(lane < N_GROUPS, logits, neg)
    gmax = jnp.max(gl, axis=-1, keepdims=True)
    gidx = jnp.min(jnp.where(gl == gmax, lane, LANES), axis=-1, keepdims=True)
    grp_gate = 1.0 / jnp.sum(jnp.where(lane < N_GROUPS, jnp.exp(logits - gmax), 0.0), axis=-1, keepdims=True)
    lo = N_GROUPS + EXPERTS_PER_GROUP * gidx
    fl = jnp.where((lane >= lo) & (lane < lo + EXPERTS_PER_GROUP), logits, neg)
    v1 = jnp.max(fl, axis=-1, keepdims=True)
    i1 = jnp.min(jnp.where(fl == v1, lane, LANES), axis=-1, keepdims=True)
    fl2 = jnp.where(lane == i1, neg, fl)
    v2 = jnp.max(fl2, axis=-1, keepdims=True)
    i2 = jnp.min(jnp.where(fl2 == v2, lane, LANES), axis=-1, keepdims=True)
    t = jnp.exp(v2 - v1)
    g1 = grp_gate / (1.0 + t)
    g2 = grp_gate * t / (1.0 + t)
    e1 = (i1 - N_GROUPS).astype(F32)
    e2 = (i2 - N_GROUPS).astype(F32)
    route_ref[rows, :] = jnp.where(lane == 0, e1, jnp.where(lane == 1, e2, jnp.where(lane == 2, g1, jnp.where(lane == 3, g2, 0.0))))


def _mix(x2d, na, pm, qm_col, dil_outs, kvm, w_out_bf16, g_ffn, w_r, b_r, s):
    n, d = x2d.shape
    tm = ROW_TILE
    tpb = s // tm
    dils = tuple(dil for _, dil in DIL_PAIRS)
    w_r_hi = w_r.astype(BF16)
    w_r_lo = (w_r - w_r_hi.astype(F32)).astype(BF16)
    dil_specs, dil_args = [], []
    for (o, lse), dil in zip(dil_outs, dils):
        spec = pl.BlockSpec((1, dil, tm // dil, DIL_W), lambda i: (i // tpb, 0, i % tpb, 0))
        dil_specs += [spec, spec]
        dil_args += [o, lse]
    return pl.pallas_call(
        functools.partial(_mix_kernel, tm=tm, dils=dils),
        grid=(n // tm,),
        in_specs=[
            pl.BlockSpec((tm, d), lambda i: (i, 0)),
            pl.BlockSpec((tm, NA_W), lambda i: (i, 0)),
            pl.BlockSpec((tm, MEM_W), lambda i: (i, qm_col // MEM_W)),
            *dil_specs,
            pl.BlockSpec((1, kvm.shape[1], kvm.shape[2]), lambda i: (i // tpb, 0, 0)),
            pl.BlockSpec(w_out_bf16.shape, lambda i: (0, 0)),
            pl.BlockSpec((1, d), lambda i: (0, 0)),
            pl.BlockSpec(w_r.shape, lambda i: (0, 0)),
            pl.BlockSpec(w_r.shape, lambda i: (0, 0)),
            pl.BlockSpec(b_r.shape, lambda i: (0, 0)),
        ],
        out_specs=(
            pl.BlockSpec((tm, d), lambda i: (i, 0)),
            pl.BlockSpec((tm, d // 2), lambda i: (i, 0)),
            pl.BlockSpec((tm, LANES), lambda i: (i, 0)),
        ),
        out_shape=(
            jax.ShapeDtypeStruct((n, d), F32),
            jax.ShapeDtypeStruct((n, d // 2), U32),
            jax.ShapeDtypeStruct((n, LANES), F32),
        ),
        scratch_shapes=[pltpu.VMEM((tm // MIX_SUBTILES, LANES), F32)]
        * (MIX_SUBTILES * 2 * sum(dil > 1 for dil in dils) * (DIL_W // LANES)),
        compiler_params=_cparams(1),
        name="merge_memattn_outproj_router",
    )(x2d, na, pm, *dil_args, kvm, w_out_bf16, g_ffn.reshape(1, d), w_r_hi, w_r_lo, b_r)


def _dispatch_kernel(dest_hbm, h_ref, xs_init_hbm, xs_hbm, idx_smem, idx_sem, row_sem, *, tm):
    del xs_init_hbm
    i = pl.program_id(0)
    n_steps = pl.num_programs(0)
    slot = i % 2

    def idx_copy(step, sl):
        return pltpu.make_async_copy(dest_hbm.at[step], idx_smem.at[sl], idx_sem.at[sl])

    def row_copy(t, d):
        return pltpu.make_async_copy(h_ref.at[pl.ds(t, 1)], xs_hbm.at[pl.ds(d, 1)], row_sem)

    @pl.when(i == 0)
    def _():
        idx_copy(0, 0).start()

    def step(sl):
        idx_copy(i, sl).wait()

        @pl.when(i + 1 < n_steps)
        def _():
            idx_copy(i + 1, 1 - sl).start()

        def start(t, carry):
            row_copy(t, idx_smem[sl, 2 * t]).start(priority=0)
            row_copy(t, idx_smem[sl, 2 * t + 1]).start(priority=1)
            return carry

        def wait(t, carry):
            row_copy(0, 0).wait()
            row_copy(0, 0).wait()
            return carry

        lax.fori_loop(0, tm, start, 0, unroll=ROW_DMA_UNROLL)
        lax.fori_loop(0, tm, wait, 0, unroll=ROW_DMA_UNROLL)

    for sl in range(2):
        pl.when(slot == sl)(functools.partial(step, sl))


def _dispatch(dest2d, h_packed, cap, tm):
    n, w = h_packed.shape
    xs_init = jnp.zeros((cap, w), U32)
    return pl.pallas_call(
        functools.partial(_dispatch_kernel, tm=tm),
        grid=(n // tm,),
        in_specs=[
            pl.BlockSpec(memory_space=pl.ANY),
            pl.BlockSpec((tm, w), lambda i: (i, 0)),
            pl.BlockSpec(memory_space=pl.ANY),
        ],
        out_specs=pl.BlockSpec(memory_space=pl.ANY),
        out_shape=jax.ShapeDtypeStruct((cap, w), U32),
        scratch_shapes=[pltpu.SMEM((2, 2 * tm), I32), pltpu.SemaphoreType.DMA((2,)), pltpu.SemaphoreType.DMA],
        input_output_aliases={2: 0},
        compiler_params=_cparams(1),
        name="moe_dispatch",
    )(dest2d, h_packed, xs_init)


def _expert_kernel(blk_exp_ref, n_used_ref, xs_ref, w1_ref, w3_ref, w2_ref, yb_ref):
    del blk_exp_ref

    @pl.when(pl.program_id(0) < n_used_ref[0])
    def _():
        a, b = _unpack_pair(xs_ref[...])
        a, b = a.astype(BF16), b.astype(BF16)
        half = a.shape[1]
        h1 = (jnp.dot(a, w1_ref[0, :half, :], preferred_element_type=F32)
              + jnp.dot(b, w1_ref[0, half:, :], preferred_element_type=F32))
        h3 = (jnp.dot(a, w3_ref[0, :half, :], preferred_element_type=F32)
              + jnp.dot(b, w3_ref[0, half:, :], preferred_element_type=F32))
        act = (h1 * jax.nn.sigmoid(h1) * h3).astype(BF16)
        y = jnp.dot(act, w2_ref[0], preferred_element_type=F32)
        yb_ref[...] = _pack_pair(y[:, :half], y[:, half:])

    @pl.when(pl.program_id(0) >= n_used_ref[0])
    def _():
        yb_ref[...] = jnp.zeros(yb_ref.shape, yb_ref.dtype)


def _experts(blk_exp, n_used, xs, w1, w3, w2):
    cap, w = xs.shape
    n_blk = cap // MOE_BLOCK
    d, de = w1.shape[1], w1.shape[2]
    row = lambda i, be, nu: (jnp.minimum(i, nu[0] - 1), 0)
    wsel = lambda i, be, nu: (be[jnp.minimum(i, nu[0] - 1)], 0, 0)
    return pl.pallas_call(
        _expert_kernel,
        grid_spec=pltpu.PrefetchScalarGridSpec(
            num_scalar_prefetch=2,
            grid=(n_blk,),
            in_specs=[
                pl.BlockSpec((MOE_BLOCK, w), row),
                pl.BlockSpec((1, d, de), wsel),
                pl.BlockSpec((1, d, de), wsel),
                pl.BlockSpec((1, de, d), wsel),
            ],
            out_specs=pl.BlockSpec((MOE_BLOCK, w), lambda i, be, nu: (i, 0)),
        ),
        out_shape=jax.ShapeDtypeStruct((cap, w), U32),
        compiler_params=_cparams(1),
        name="moe_experts",
    )(blk_exp, n_used, xs, w1, w3, w2)


def _combine_kernel(dest_hbm, yb_hbm, x1_ref, route_ref, o_ref, idx_smem, ybuf, idx_sem, row_sem, *, tm):
    i = pl.program_id(0)
    n_steps = pl.num_programs(0)
    slot = i % 2

    def idx_copy(step, sl):
        return pltpu.make_async_copy(dest_hbm.at[step], idx_smem.at[sl], idx_sem.at[sl])

    def row_copy(sl, t, k, d):
        return pltpu.make_async_copy(yb_hbm.at[pl.ds(d, 1)], ybuf.at[sl, k, pl.ds(t, 1)], row_sem.at[sl])

    def issue(sl):
        def start(t, carry):
            row_copy(sl, t, 0, idx_smem[sl, 2 * t]).start(priority=0)
            row_copy(sl, t, 1, idx_smem[sl, 2 * t + 1]).start(priority=1)
            return carry

        lax.fori_loop(0, tm, start, 0, unroll=ROW_DMA_UNROLL)

    @pl.when(i == 0)
    def _():
        idx_copy(0, 0).start()
        idx_copy(0, 0).wait()
        issue(0)

        @pl.when(n_steps > 1)
        def _():
            idx_copy(1, 1).start()

    def step(sl):
        @pl.when(i + 1 < n_steps)
        def _():
            idx_copy(i + 1, 1 - sl).wait()
            issue(1 - sl)

        @pl.when(i + 2 < n_steps)
        def _():
            idx_copy(i + 2, sl).start()

        def wait(t, carry):
            row_copy(sl, 0, 0, 0).wait()
            row_copy(sl, 0, 1, 0).wait()
            return carry

        lax.fori_loop(0, tm, wait, 0, unroll=ROW_DMA_UNROLL)
        a0, b0 = _unpack_pair(ybuf[sl, 0])
        a1, b1 = _unpack_pair(ybuf[sl, 1])
        g0 = route_ref[:, 2:3]
        g1 = route_ref[:, 3:4]
        half = a0.shape[1]
        o_ref[:, :half] = x1_ref[:, :half] + (g0 * a0 + g1 * a1)
        o_ref[:, half:] = x1_ref[:, half:] + (g0 * b0 + g1 * b1)

    for sl in range(2):
        pl.when(slot == sl)(functools.partial(step, sl))


def _combine(dest2d, yb, x1, route, tm):
    n, d = x1.shape
    w = yb.shape[1]
    return pl.pallas_call(
        functools.partial(_combine_kernel, tm=tm),
        grid=(n // tm,),
        in_specs=[
            pl.BlockSpec(memory_space=pl.ANY),
            pl.BlockSpec(memory_space=pl.ANY),
            pl.BlockSpec((tm, d), lambda i: (i, 0)),
            pl.BlockSpec((tm, LANES), lambda i: (i, 0)),
        ],
        out_specs=pl.BlockSpec((tm, d), lambda i: (i, 0)),
        out_shape=jax.ShapeDtypeStruct((n, d), F32),
        scratch_shapes=[pltpu.SMEM((2, 2 * tm), I32), pltpu.VMEM((2, 2, tm, w), U32),
                        pltpu.SemaphoreType.DMA((2,)), pltpu.SemaphoreType.DMA((2,))],
        compiler_params=_cparams(1),
        name="moe_combine",
    )(dest2d, yb, x1, route)


def _routing_plan(route, n):
    e_flat = route[:, :2].astype(I32).reshape(-1)
    blk = RANK_BLOCK
    onehot = (e_flat[:, None] == jnp.arange(N_EXPERTS, dtype=I32)[None, :]).astype(BF16).reshape(-1, blk, N_EXPERTS)
    tri = (jnp.arange(blk)[:, None] > jnp.arange(blk)[None, :]).astype(BF16)
    within = jnp.einsum('ij,bjk->bik', tri, onehot, preferred_element_type=F32)
    blk_tot = jnp.sum(onehot.astype(F32), axis=1)
    blk_base = jnp.cumsum(blk_tot, axis=0) - blk_tot
    rank = jnp.sum((within + blk_base[:, None, :]) * onehot.astype(F32), axis=-1).reshape(-1).astype(I32)
    counts = jnp.sum(blk_tot, axis=0).astype(I32)
    padded = (counts + MOE_BLOCK - 1) // MOE_BLOCK * MOE_BLOCK
    pad_end = jnp.cumsum(padded)
    pad_start = pad_end - padded
    dest = pad_start[e_flat] + rank
    cap = 2 * n + N_EXPERTS * MOE_BLOCK
    n_blk = cap // MOE_BLOCK
    blk_pos = jnp.arange(n_blk, dtype=I32) * MOE_BLOCK
    blk_exp = jnp.minimum(jnp.sum((pad_end[None, :] <= blk_pos[:, None]).astype(I32), axis=1), N_EXPERTS - 1)
    n_used = (pad_end[-1:] // MOE_BLOCK).astype(I32)
    return dest.astype(I32), blk_exp, n_used, cap


def _layer(x, mem, g_mix, w_in, qk_gain, na_rpb, t5, g_mem, w_mem_kv, w_out, g_ffn, w_r1, b_r1, w_r2, b_r2, w1, w3, w2):
    b, s, d = x.shape
    n = b * s
    tm = PROJ_TILE
    scale = HEAD_DIM ** -0.5
    dils = tuple(dil for _, dil in DIL_PAIRS)

    o_qd, o_kd, o_vd, o_qm = 3 * NA_W, 3 * NA_W + 3 * DIL_W, 3 * NA_W + 6 * DIL_W, 3 * NA_W + 9 * DIL_W
    w_bf = w_in.astype(BF16)
    col_blocks = [w_bf[:, :3 * NA_W], w_bf[:, o_qm:o_qm + MEM_W]]
    for g in range(len(DIL_PAIRS)):
        for base in (o_qd, o_kd, o_vd):
            col_blocks.append(w_bf[:, base + g * DIL_W:base + (g + 1) * DIL_W])
    w_perm = jnp.concatenate(col_blocks, axis=1)
    tile = lambda v: jnp.tile(v.astype(F32), CHUNK // HEAD_DIM)
    ones = jnp.ones((CHUNK,), F32)
    qa_g, ka_g = tile(qk_gain[0, 0]) * scale, tile(qk_gain[0, 1])
    qd_g, kd_g = tile(qk_gain[1, 0]) * scale, tile(qk_gain[1, 1])
    qm_g, km_g = tile(qk_gain[2, 0]) * scale, tile(qk_gain[2, 1])
    main_w = 3 * NA_W + MEM_W
    plan = [(True, 0, 0, 1), (True, 0, CHUNK, 1), (True, 0, 2 * CHUNK, 1), (True, 0, 3 * CHUNK, 1),
            (False, 0, 4 * CHUNK, 1), (False, 0, 5 * CHUNK, 1), (True, 0, 6 * CHUNK, 1)]
    gains = [qa_g, qa_g, ka_g, ka_g, ones, ones, qm_g]
    for gi, dil in enumerate(dils):
        plan += [(True, 1 + gi, 0, dil), (True, 1 + gi, DIL_W, dil), (False, 1 + gi, 2 * DIL_W, dil)]
        gains += [qd_g, kd_g, ones]
    tpb = s // tm
    out_shapes = [jax.ShapeDtypeStruct((n, main_w), BF16)]
    out_specs = [pl.BlockSpec((tm, main_w), lambda i: (i, 0))]
    for dil in dils:
        if dil == 1:
            out_shapes.append(jax.ShapeDtypeStruct((n, 3 * DIL_W), BF16))
            out_specs.append(pl.BlockSpec((tm, 3 * DIL_W), lambda i: (i, 0)))
        else:
            out_shapes.append(jax.ShapeDtypeStruct((b, dil, s // dil, 3 * DIL_W), BF16))
            out_specs.append(pl.BlockSpec((1, dil, tm // dil, 3 * DIL_W), lambda i: (i // tpb, 0, i % tpb, 0)))
    pm, *dgs = _project(x.reshape(n, d), g_mix, w_perm, jnp.stack(gains), tuple(plan), out_shapes, out_specs, tm)

    m_rows = mem.shape[0] * mem.shape[1]
    tmm = min(tm, m_rows)
    kv_plan = ((True, 0, 0, 1), (False, 0, CHUNK, 1))
    kvm = _project(mem.reshape(m_rows, d), g_mem, w_mem_kv.astype(BF16), jnp.stack([km_g, ones]), kv_plan,
                   [jax.ShapeDtypeStruct((m_rows, 2 * MEM_W), BF16)],
                   [pl.BlockSpec((tmm, 2 * MEM_W), lambda i: (i, 0))], tmm)[0]
    kvm = kvm.reshape(mem.shape[0], mem.shape[1], 2 * MEM_W)

    out_na = _neighbourhood_attention(pm.reshape(b, s, main_w), _na_bias_table(na_rpb), b, s)

    t5g = t5.reshape(T5_BUCKETS, len(DIL_PAIRS), DIL_HEADS_PER_GROUP)
    dil_outs = []
    for gi, (dg, dil) in enumerate(zip(dgs, dils)):
        dg = dg.reshape(b, dil, s // dil, 3 * DIL_W)
        dil_outs.append(_dilated_attention(dg, _dil_band_table(t5g, gi, dil), b, dil, s // dil))

    w_r = jnp.zeros((d, LANES), F32).at[:, :N_GROUPS].set(w_r1.astype(F32)).at[:, N_GROUPS:N_GROUPS + N_EXPERTS].set(w_r2.astype(F32))
    b_r = jnp.zeros((1, LANES), F32).at[0, :N_GROUPS].set(b_r1.astype(F32)).at[0, N_GROUPS:N_GROUPS + N_EXPERTS].set(b_r2.astype(F32))
    x1, h_packed, route = _mix(x.reshape(n, d), out_na.reshape(n, NA_W), pm, 3 * NA_W, dil_outs, kvm,
                               w_out.astype(BF16), g_ffn, w_r, b_r, s)

    dest, blk_exp, n_used, cap = _routing_plan(route, n)
    tmd = 256
    dest2d = dest.reshape(n // tmd, 2 * tmd)
    xs = _dispatch(dest2d, h_packed, cap, tmd)
    yb = _experts(blk_exp, n_used, xs, w1.astype(BF16), w3.astype(BF16), w2.astype(BF16))
    out = _combine(dest2d, yb, x1, route, tmd)
    return out.reshape(b, s, d)


def kernel(x, mem, g_mix, w_in, qk_gain, na_rpb, t5_table, g_mem, w_mem_kv, w_out, g_ffn, w_r1, b_r1, w_r2, b_r2, w1, w3, w2):
    for l in range(g_mix.shape[0]):
        x = _layer(x, mem, g_mix[l], w_in[l], qk_gain[l], na_rpb[l], t5_table, g_mem[l], w_mem_kv[l], w_out[l],
                   g_ffn[l], w_r1[l], b_r1[l], w_r2[l], b_r2[l], w1[l], w3[l], w2[l])
    return x
```

```python
import functools
import math

import jax
import jax.numpy as jnp
from jax import lax
from jax.experimental import pallas as pl
from jax.experimental.pallas import tpu as pltpu

F32 = jnp.float32
BF16 = jnp.bfloat16
U32 = jnp.uint32
I32 = jnp.int32

HEAD_DIM = 64
GRID_W = 64
NA_HEADS = 8
NA_WIN_ROWS = 8
NA_WIN_COLS = 16
NA_GROUP_ROWS = 4
NA_KEY_ROWS = NA_WIN_ROWS + NA_GROUP_ROWS - 1
DIL_PAIRS = ((128, 1), (512, 4), (2048, 16))
DIL_HEADS_PER_GROUP = 4
DIL_SIDE = 64
DIL_BLOCK = 128
DIL_UNROLL = 4
MEM_HEADS = 4
T5_BUCKETS = 32
T5_MAX_DIST = 1024
N_GROUPS = 4
EXPERTS_PER_GROUP = 8
N_EXPERTS = N_GROUPS * EXPERTS_PER_GROUP
MOE_BLOCK = 512
RANK_BLOCK = 256
EPS = 1e-6
NEG_INF = -1e30

NA_W = NA_HEADS * HEAD_DIM
DIL_W = DIL_HEADS_PER_GROUP * HEAD_DIM
MEM_W = MEM_HEADS * HEAD_DIM
LANES = 128
CHUNK = 256
ROW_TILE = 512
PROJ_TILE = 1024
VMEM_LIMIT = 56 * 1024 * 1024
MIX_SUBTILES = 2
MOE_ROW_TILE = 512
NA_GROUPS_PER_STEP = 4
ROW_DMA_UNROLL = 8


def _cparams(n_axes):
    return pltpu.CompilerParams(dimension_semantics=("arbitrary",) * n_axes, vmem_limit_bytes=VMEM_LIMIT)


def _nt_dot(a, b):
    return lax.dot_general(a, b, (((1,), (1,)), ((), ())), preferred_element_type=F32)


def _pack_pair(a, b):
    ua = lax.bitcast_convert_type(a.astype(BF16).astype(F32), U32)
    ub = lax.bitcast_convert_type(b.astype(BF16).astype(F32), U32)
    return (ua >> 16) | (ub & jnp.uint32(0xFFFF0000))


def _unpack_pair(p):
    a = lax.bitcast_convert_type(p << 16, F32)
    b = lax.bitcast_convert_type(p & jnp.uint32(0xFFFF0000), F32)
    return a, b


def _proj_kernel(x_ref, g_ref, w_ref, gain_ref, bd_ref, *refs, plan, tm):
    n_buf = CHUNK // LANES
    out_refs, ybufs = refs[:-n_buf], refs[-n_buf:]
    x = x_ref[...]
    ms = jnp.mean(x * x, axis=-1, keepdims=True)
    h = (x * lax.rsqrt(ms + EPS) * g_ref[...]).astype(BF16)
    for c, (normed, oi, col, dil) in enumerate(plan):
        y = jnp.dot(h, w_ref[:, c * CHUNK:(c + 1) * CHUNK], preferred_element_type=F32)
        if normed:
            msq = jnp.dot((y * y).astype(BF16), bd_ref[...], preferred_element_type=F32)
            y = y * lax.rsqrt(msq + EPS) * gain_ref[c:c + 1, :]
        o_ref = out_refs[oi]
        if dil == 1:
            o_ref[:, col:col + CHUNK] = y.astype(o_ref.dtype)
        else:
            for j, ybuf in enumerate(ybufs):
                ybuf[...] = y[:, j * LANES:(j + 1) * LANES]
            for rho in range(dil):
                for j, ybuf in enumerate(ybufs):
                    rows = ybuf[pl.ds(rho, tm // dil, stride=dil), :]
                    o_ref[0, rho, :, col + j * LANES:col + (j + 1) * LANES] = rows.astype(o_ref.dtype)


def _project(x2d, g, w_bf16, gains, plan, out_shapes, out_specs, tm):
    n, d = x2d.shape
    cols = w_bf16.shape[1]
    bd = jnp.kron(jnp.eye(CHUNK // HEAD_DIM, dtype=F32), jnp.full((HEAD_DIM, HEAD_DIM), 1.0 / HEAD_DIM, F32)).astype(BF16)
    return pl.pallas_call(
        functools.partial(_proj_kernel, plan=plan, tm=tm),
        grid=(n // tm,),
        in_specs=[
            pl.BlockSpec((tm, d), lambda i: (i, 0)),
            pl.BlockSpec((1, d), lambda i: (0, 0)),
            pl.BlockSpec((d, cols), lambda i: (0, 0)),
            pl.BlockSpec(gains.shape, lambda i: (0, 0)),
            pl.BlockSpec((CHUNK, CHUNK), lambda i: (0, 0)),
        ],
        out_specs=out_specs,
        out_shape=out_shapes,
        scratch_shapes=[pltpu.VMEM((tm, LANES), F32)] * (CHUNK // LANES),
        compiler_params=_cparams(1),
        name="rmsnorm_project",
    )(x2d, g.reshape(1, d), w_bf16, gains, bd)


def _na_kernel(q_ref, k_ref, v_ref, bias_ref, o_ref, *, groups_per_step, n_rows):
    step = pl.program_id(2)
    n_groups = n_rows // NA_GROUP_ROWS
    tq = NA_GROUP_ROWS * GRID_W
    keys = NA_KEY_ROWS * GRID_W
    for i in range(groups_per_step):
        rg = step * groups_per_step + i
        key_row0 = jnp.clip(rg * NA_GROUP_ROWS - NA_WIN_ROWS // 2, 0, n_rows - NA_KEY_ROWS)
        gtype = jnp.where(rg == 0, 0, jnp.where(rg == n_groups - 1, 2, 1))
        k0 = pl.multiple_of(key_row0 * GRID_W, GRID_W)
        q = q_ref[0, i * tq:(i + 1) * tq, :]
        k = k_ref[0, pl.ds(k0, keys), :]
        v = v_ref[0, pl.ds(k0, keys), :]
        outs = []
        for hh in range(LANES // HEAD_DIM):
            sl = slice(hh * HEAD_DIM, (hh + 1) * HEAD_DIM)
            s = _nt_dot(q[:, sl], k[:, sl]) + bias_ref[hh, gtype]
            m = jnp.max(s, axis=-1, keepdims=True)
            e = jnp.exp(s - m)
            den = jnp.sum(e, axis=-1, keepdims=True)
            o = jnp.dot(e.astype(BF16), v[:, sl], preferred_element_type=F32)
            outs.append(o / den)
        o_ref[0, i * tq:(i + 1) * tq, :] = jnp.concatenate(outs, axis=-1).astype(o_ref.dtype)


def _toeplitz(g, rows, cols):
    p = g.shape[-1]
    tiled = jnp.tile(g, (1,) * (g.ndim - 1) + (rows,))[..., :rows * (p - 1)]
    return tiled.reshape(g.shape[:-1] + (rows, p - 1))[..., :cols]


def _na_bias_table(rpb):
    kh, kw = NA_WIN_ROWS, NA_WIN_COLS
    h = rpb.shape[0]
    rpb = rpb.astype(F32)
    by_row = jnp.stack([rpb[:, kh - 1 - t:2 * kh - 1 - t, :] for t in range(kh)], axis=1)
    lo = GRID_W - kw
    padded = jnp.pad(by_row, ((0, 0), (0, 0), (0, 0), (lo, 2 * GRID_W - lo - (2 * kw - 1))), constant_values=NEG_INF)
    vals = _toeplitz(jnp.roll(padded, -(GRID_W - 1), axis=-1), GRID_W, GRID_W)
    c = jnp.arange(GRID_W)[:, None]
    kc = jnp.arange(GRID_W)[None, :]
    c0 = jnp.clip(c - kw // 2, 0, GRID_W - kw)
    inside = (kc >= c0) & (kc < c0 + kw)
    per_row = jnp.where(inside, vals, NEG_INF).transpose(0, 1, 3, 2, 4)

    def placed(t, shift):
        return jnp.pad(per_row[:, t], ((0, 0), (0, 0), (shift, NA_KEY_ROWS - kh - shift), (0, 0)), constant_values=NEG_INF)

    mid = kh // 2
    top = jnp.stack([placed(i, 0) for i in range(NA_GROUP_ROWS)], axis=1)
    interior = jnp.stack([placed(mid, i) for i in range(NA_GROUP_ROWS)], axis=1)
    bottom = jnp.stack([placed(mid + i, NA_KEY_ROWS - kh) for i in range(NA_GROUP_ROWS)], axis=1)
    table = jnp.stack([top, interior, bottom], axis=1)
    return table.reshape(h, 3, NA_GROUP_ROWS * GRID_W, NA_KEY_ROWS * GRID_W)


def _neighbourhood_attention(pm3, bias_tab, b, s):
    n_rows = s // GRID_W
    groups_per_step = NA_GROUPS_PER_STEP
    tq = groups_per_step * NA_GROUP_ROWS * GRID_W
    pairs = NA_W // LANES
    assert n_rows % (groups_per_step * NA_GROUP_ROWS) == 0 and n_rows >= NA_KEY_ROWS + NA_GROUP_ROWS
    return pl.pallas_call(
        functools.partial(_na_kernel, groups_per_step=groups_per_step, n_rows=n_rows),
        grid=(b, pairs, n_rows // (groups_per_step * NA_GROUP_ROWS)),
        in_specs=[
            pl.BlockSpec((1, tq, LANES), lambda bi, hp, rb: (bi, rb, hp)),
            pl.BlockSpec((1, s, LANES), lambda bi, hp, rb: (bi, 0, pairs + hp)),
            pl.BlockSpec((1, s, LANES), lambda bi, hp, rb: (bi, 0, 2 * pairs + hp)),
            pl.BlockSpec((LANES // HEAD_DIM, 3, NA_GROUP_ROWS * GRID_W, NA_KEY_ROWS * GRID_W), lambda bi, hp, rb: (hp, 0, 0, 0)),
        ],
        out_specs=pl.BlockSpec((1, tq, LANES), lambda bi, hp, rb: (bi, rb, hp)),
        out_shape=jax.ShapeDtypeStruct((b, s, NA_W), BF16),
        compiler_params=_cparams(3),
        name="neighbourhood_attention",
    )(pm3, pm3, pm3, bias_tab)


def _dil_kernel(q_ref, k_ref, v_ref, band_ref, o_ref, lse_ref, *, length):
    nblk = length // DIL_BLOCK
    span = DIL_BLOCK + 2 * DIL_SIDE

    def blk(i, carry):
        i0 = pl.multiple_of(i * DIL_BLOCK, DIL_BLOCK)
        p0 = pl.multiple_of(jnp.maximum(i0 - DIL_SIDE, 0), DIL_SIDE)
        n0 = pl.multiple_of(jnp.minimum(i0 + DIL_BLOCK, length - DIL_SIDE), DIL_SIDE)
        q = q_ref[0, 0, pl.ds(i0, DIL_BLOCK), :]
        k = jnp.concatenate([k_ref[0, 0, pl.ds(p0, DIL_SIDE), :], k_ref[0, 0, pl.ds(i0, DIL_BLOCK), :],
                             k_ref[0, 0, pl.ds(n0, DIL_SIDE), :]], axis=0)
        v = jnp.concatenate([v_ref[0, 0, pl.ds(p0, DIL_SIDE), :], v_ref[0, 0, pl.ds(i0, DIL_BLOCK), :],
                             v_ref[0, 0, pl.ds(n0, DIL_SIDE), :]], axis=0)
        kpos = i0 - DIL_SIDE + lax.broadcasted_iota(I32, (1, span), 1)
        valid = (kpos >= 0) & (kpos < length)
        outs, lses = [], []
        for hh in range(LANES // HEAD_DIM):
            sl = slice(hh * HEAD_DIM, (hh + 1) * HEAD_DIM)
            s = _nt_dot(q[:, sl], k[:, sl]) + band_ref[hh]
            s = jnp.where(valid, s, NEG_INF)
            m = jnp.max(s, axis=-1, keepdims=True)
            e = jnp.exp(s - m)
            den = jnp.sum(e, axis=-1, keepdims=True)
            o = jnp.dot(e.astype(BF16), v[:, sl], preferred_element_type=F32)
            outs.append(o / den)
            lses.append(jnp.broadcast_to(m + jnp.log(den), (DIL_BLOCK, HEAD_DIM)))
        o_ref[0, 0, pl.ds(i0, DIL_BLOCK), :] = jnp.concatenate(outs, axis=-1)
        lse_ref[0, 0, pl.ds(i0, DIL_BLOCK), :] = jnp.concatenate(lses, axis=-1)
        return carry

    unroll = math.gcd(nblk, DIL_UNROLL)

    def blk_group(j, carry):
        for u in range(unroll):
            blk(j * unroll + u, carry)
        return carry

    lax.fori_loop(0, nblk // unroll, blk_group, 0)


def _t5_bucket(rel):
    nb = T5_BUCKETS // 2
    max_exact = nb // 2
    n = jnp.abs(rel)
    upper = (rel > 0).astype(I32) * nb
    nf = jnp.maximum(n, 1).astype(F32)
    large = max_exact + (jnp.log(nf / max_exact) / math.log(T5_MAX_DIST / max_exact) * (nb - max_exact)).astype(I32)
    large = jnp.minimum(large, nb - 1)
    return upper + jnp.where(n < max_exact, n, large)


def _dil_band_table(t5, g, dil):
    offs = jnp.arange(-DIL_SIDE, DIL_SIDE + 1) * dil
    bias = t5[_t5_bucket(offs), g].T.astype(F32)
    span = DIL_BLOCK + 2 * DIL_SIDE
    period = DIL_BLOCK + span
    g_vec = jnp.pad(bias, ((0, 0), (0, period - bias.shape[1])), constant_values=NEG_INF)
    return _toeplitz(g_vec, DIL_BLOCK, span)


def _dilated_attention(dg, band, b, dil, length):
    pairs = DIL_W // LANES
    spec = lambda off: pl.BlockSpec((1, 1, length, LANES), lambda bi, rho, hp: (bi, rho, 0, off + hp))
    out_spec = pl.BlockSpec((1, 1, length, LANES), lambda bi, rho, hp: (bi, rho, 0, hp))
    shape = jax.ShapeDtypeStruct((b, dil, length, DIL_W), F32)
    return pl.pallas_call(
        functools.partial(_dil_kernel, length=length),
        grid=(b, dil, pairs),
        in_specs=[spec(0), spec(pairs), spec(2 * pairs),
                  pl.BlockSpec((LANES // HEAD_DIM, DIL_BLOCK, DIL_BLOCK + 2 * DIL_SIDE), lambda bi, rho, hp: (hp, 0, 0))],
        out_specs=(out_spec, out_spec),
        out_shape=(shape, shape),
        compiler_params=_cparams(3),
        name=f"dilated_attention_d{dil}",
    )(dg, dg, dg, band)


def _mix_kernel(x_ref, na_ref, qm_ref, o0_ref, l0_ref, o1_ref, l1_ref, o2_ref, l2_ref, kvm_ref, wout_ref, gffn_ref,
                wrh_ref, wrl_ref, br_ref, x1_ref, hp_ref, route_ref, *ibufs, tm, dils):
    sub = tm // MIX_SUBTILES
    bufs = iter(ibufs)
    for si in range(MIX_SUBTILES):
        _mix_rows(si, sub, bufs, x_ref, na_ref, qm_ref, (o0_ref, o1_ref, o2_ref), (l0_ref, l1_ref, l2_ref), kvm_ref,
                  wout_ref, gffn_ref, wrh_ref, wrl_ref, br_ref, x1_ref, hp_ref, route_ref, dils)


def _mix_rows(si, sub, bufs, x_ref, na_ref, qm_ref, o_refs, l_refs, kvm_ref, wout_ref, gffn_ref, wrh_ref, wrl_ref,
              br_ref, x1_ref, hp_ref, route_ref, dils):
    rows = slice(si * sub, (si + 1) * sub)

    def token_major(ref, dil):
        if dil == 1:
            return ref[0, 0, rows, :]
        n_cls = sub // dil
        mine = [next(bufs) for _ in range(DIL_W // LANES)]
        for rho in range(dil):
            for j, ibuf in enumerate(mine):
                ibuf[pl.ds(rho, n_cls, stride=dil), :] = ref[0, rho, si * n_cls:(si + 1) * n_cls, j * LANES:(j + 1) * LANES]
        return jnp.concatenate([ibuf[...] for ibuf in mine], axis=-1)

    lses = [token_major(l_ref, dil) for l_ref, dil in zip(l_refs, dils)]
    lmax = jnp.maximum(jnp.maximum(lses[0], lses[1]), lses[2])
    wts = [jnp.exp(l - lmax) for l in lses]
    num = None
    for o_ref, dil, w in zip(o_refs, dils, wts):
        term = w * token_major(o_ref, dil)
        num = term if num is None else num + term
    out_dil = num / (wts[0] + wts[1] + wts[2])

    qm = qm_ref[rows, :]
    mem_outs = []
    for hh in range(MEM_HEADS):
        sl = slice(hh * HEAD_DIM, (hh + 1) * HEAD_DIM)
        s = _nt_dot(qm[:, sl], kvm_ref[0, :, sl])
        m = jnp.max(s, axis=-1, keepdims=True)
        e = jnp.exp(s - m)
        den = jnp.sum(e, axis=-1, keepdims=True)
        vm = kvm_ref[0, :, MEM_W + hh * HEAD_DIM:MEM_W + (hh + 1) * HEAD_DIM]
        mem_outs.append(jnp.dot(e.astype(BF16), vm, preferred_element_type=F32) / den)
    out_mem = jnp.concatenate(mem_outs, axis=-1)

    y = jnp.dot(na_ref[rows, :], wout_ref[0:NA_W, :], preferred_element_type=F32)
    y = y + jnp.dot(out_dil.astype(BF16), wout_ref[NA_W:NA_W + DIL_W, :], preferred_element_type=F32)
    y = y + jnp.dot(out_mem.astype(BF16), wout_ref[NA_W + DIL_W:, :], preferred_element_type=F32)
    x1 = x_ref[rows, :] + y
    x1_ref[rows, :] = x1

    ms = jnp.mean(x1 * x1, axis=-1, keepdims=True)
    h = x1 * lax.rsqrt(ms + EPS) * gffn_ref[...]
    half = h.shape[1] // 2
    hp_ref[rows, :] = _pack_pair(h[:, :half], h[:, half:])
    h_hi = h.astype(BF16)
    h_lo = (h - h_hi.astype(F32)).astype(BF16)
    logits = (jnp.dot(h_hi, wrh_ref[...], preferred_element_type=F32)
              + (jnp.dot(h_hi, wrl_ref[...], preferred_element_type=F32)
                 + jnp.dot(h_lo, wrh_ref[...], preferred_element_type=F32))) + br_ref[...]
    lane = lax.broadcasted_iota(I32, logits.shape, 1)
    neg = -jnp.inf
    gl = jnp.where(lane < N_GROUPS, logits, neg)
    gmax = jnp.max(gl, axis=-1, keepdims=True)
    gidx = jnp.min(jnp.where(gl == gmax, lane, LANES), axis=-1, keepdims=True)
    grp_gate = 1.0 / jnp.sum(jnp.where(lane < N_GROUPS, jnp.exp(logits - gmax), 0.0), axis=-1, keepdims=True)
    lo = N_GROUPS + EXPERTS_PER_GROUP * gidx
    fl = jnp.where((lane >= lo) & (lane < lo + EXPERTS_PER_GROUP), logits, neg)
    v1 = jnp.max(fl, axis=-1, keepdims=True)
    i1 = jnp.min(jnp.where(fl == v1, lane, LANES), axis=-1, keepdims=True)
    fl2 = jnp.where(lane == i1, neg, fl)
    v2 = jnp.max(fl2, axis=-1, keepdims=True)
    i2 = jnp.min(jnp.where(fl2 == v2, lane, LANES), axis=-1, keepdims=True)
    t = jnp.exp(v2 - v1)
    g1 = grp_gate / (1.0 + t)
    g2 = grp_gate * t / (1.0 + t)
    e1 = (i1 - N_GROUPS).astype(F32)
    e2 = (i2 - N_GROUPS).astype(F32)
    route_ref[rows, :] = jnp.where(lane == 0, e1, jnp.where(lane == 1, e2, jnp.where(lane == 2, g1, jnp.where(lane == 3, g2, 0.0))))


def _mix(x2d, na, pm, qm_col, dil_outs, kvm, w_out_bf16, g_ffn, w_r, b_r, s):
    n, d = x2d.shape
    tm = ROW_TILE
    tpb = s // tm
    dils = tuple(dil for _, dil in DIL_PAIRS)
    w_r_hi = w_r.astype(BF16)
    w_r_lo = (w_r - w_r_hi.astype(F32)).astype(BF16)
    dil_specs, dil_args = [], []
    for (o, lse), dil in zip(dil_outs, dils):
        spec = pl.BlockSpec((1, dil, tm // dil, DIL_W), lambda i: (i // tpb, 0, i % tpb, 0))
        dil_specs += [spec, spec]
        dil_args += [o, lse]
    return pl.pallas_call(
        functools.partial(_mix_kernel, tm=tm, dils=dils),
        grid=(n // tm,),
        in_specs=[
            pl.BlockSpec((tm, d), lambda i: (i, 0)),
            pl.BlockSpec((tm, NA_W), lambda i: (i, 0)),
            pl.BlockSpec((tm, MEM_W), lambda i: (i, qm_col // MEM_W)),
            *dil_specs,
            pl.BlockSpec((1, kvm.shape[1], kvm.shape[2]), lambda i: (i // tpb, 0, 0)),
            pl.BlockSpec(w_out_bf16.shape, lambda i: (0, 0)),
            pl.BlockSpec((1, d), lambda i: (0, 0)),
            pl.BlockSpec(w_r.shape, lambda i: (0, 0)),
            pl.BlockSpec(w_r.shape, lambda i: (0, 0)),
            pl.BlockSpec(b_r.shape, lambda i: (0, 0)),
        ],
        out_specs=(
            pl.BlockSpec((tm, d), lambda i: (i, 0)),
            pl.BlockSpec((tm, d // 2), lambda i: (i, 0)),
            pl.BlockSpec((tm, LANES), lambda i: (i, 0)),
        ),
        out_shape=(
            jax.ShapeDtypeStruct((n, d), F32),
            jax.ShapeDtypeStruct((n, d // 2), U32),
            jax.ShapeDtypeStruct((n, LANES), F32),
        ),
        scratch_shapes=[pltpu.VMEM((tm // MIX_SUBTILES, LANES), F32)]
        * (MIX_SUBTILES * 2 * sum(dil > 1 for dil in dils) * (DIL_W // LANES)),
        compiler_params=_cparams(1),
        name="merge_memattn_outproj_router",
    )(x2d, na, pm, *dil_args, kvm, w_out_bf16, g_ffn.reshape(1, d), w_r_hi, w_r_lo, b_r)


def _dispatch_kernel(dest_hbm, h_ref, xs_init_hbm, xs_hbm, idx_smem, idx_sem, row_sem, *, tm):
    del xs_init_hbm
    i = pl.program_id(0)
    n_steps = pl.num_programs(0)
    slot = i % 2

    def idx_copy(step, sl):
        return pltpu.make_async_copy(dest_hbm.at[step], idx_smem.at[sl], idx_sem.at[sl])

    def row_copy(t, d):
        return pltpu.make_async_copy(h_ref.at[pl.ds(t, 1)], xs_hbm.at[pl.ds(d, 1)], row_sem)

    @pl.when(i == 0)
    def _():
        idx_copy(0, 0).start()

    def step(sl):
        idx_copy(i, sl).wait()

        @pl.when(i + 1 < n_steps)
        def _():
            idx_copy(i + 1, 1 - sl).start()

        def start(t, carry):
            row_copy(t, idx_smem[sl, 2 * t]).start(priority=0)
            row_copy(t, idx_smem[sl, 2 * t + 1]).start(priority=1)
            return carry

        def wait(t, carry):
            row_copy(0, 0).wait()
            row_copy(0, 0).wait()
            return carry

        lax.fori_loop(0, tm, start, 0, unroll=ROW_DMA_UNROLL)
        lax.fori_loop(0, tm, wait, 0, unroll=ROW_DMA_UNROLL)

    for sl in range(2):
        pl.when(slot == sl)(functools.partial(step, sl))


def _dispatch(dest2d, h_packed, cap, tm):
    n, w = h_packed.shape
    xs_init = jnp.zeros((cap, w), U32)
    return pl.pallas_call(
        functools.partial(_dispatch_kernel, tm=tm),
        grid=(n // tm,),
        in_specs=[
            pl.BlockSpec(memory_space=pl.ANY),
            pl.BlockSpec((tm, w), lambda i: (i, 0)),
            pl.BlockSpec(memory_space=pl.ANY),
        ],
        out_specs=pl.BlockSpec(memory_space=pl.ANY),
        out_shape=jax.ShapeDtypeStruct((cap, w), U32),
        scratch_shapes=[pltpu.SMEM((2, 2 * tm), I32), pltpu.SemaphoreType.DMA((2,)), pltpu.SemaphoreType.DMA],
        input_output_aliases={2: 0},
        compiler_params=_cparams(1),
        name="moe_dispatch",
    )(dest2d, h_packed, xs_init)


def _expert_kernel(blk_exp_ref, n_used_ref, xs_ref, w1_ref, w3_ref, w2_ref, yb_ref):
    del blk_exp_ref

    @pl.when(pl.program_id(0) < n_used_ref[0])
    def _():
        a, b = _unpack_pair(xs_ref[...])
        a, b = a.astype(BF16), b.astype(BF16)
        half = a.shape[1]
        h1 = (jnp.dot(a, w1_ref[0, :half, :], preferred_element_type=F32)
              + jnp.dot(b, w1_ref[0, half:, :], preferred_element_type=F32))
        h3 = (jnp.dot(a, w3_ref[0, :half, :], preferred_element_type=F32)
              + jnp.dot(b, w3_ref[0, half:, :], preferred_element_type=F32))
        act = (h1 * jax.nn.sigmoid(h1) * h3).astype(BF16)
        y = jnp.dot(act, w2_ref[0], preferred_element_type=F32)
        yb_ref[...] = _pack_pair(y[:, :half], y[:, half:])

    @pl.when(pl.program_id(0) >= n_used_ref[0])
    def _():
        yb_ref[...] = jnp.zeros(yb_ref.shape, yb_ref.dtype)


def _experts(blk_exp, n_used, xs, w1, w3, w2):
    cap, w = xs.shape
    n_blk = cap // MOE_BLOCK
    d, de = w1.shape[1], w1.shape[2]
    row = lambda i, be, nu: (jnp.minimum(i, nu[0] - 1), 0)
    wsel = lambda i, be, nu: (be[jnp.minimum(i, nu[0] - 1)], 0, 0)
    return pl.pallas_call(
        _expert_kernel,
        grid_spec=pltpu.PrefetchScalarGridSpec(
            num_scalar_prefetch=2,
            grid=(n_blk,),
            in_specs=[
                pl.BlockSpec((MOE_BLOCK, w), row),
                pl.BlockSpec((1, d, de), wsel),
                pl.BlockSpec((1, d, de), wsel),
                pl.BlockSpec((1, de, d), wsel),
            ],
            out_specs=pl.BlockSpec((MOE_BLOCK, w), lambda i, be, nu: (i, 0)),
        ),
        out_shape=jax.ShapeDtypeStruct((cap, w), U32),
        compiler_params=_cparams(1),
        name="moe_experts",
    )(blk_exp, n_used, xs, w1, w3, w2)


def _combine_kernel(dest_hbm, yb_hbm, x1_ref, route_ref, o_ref, idx_smem, ybuf, idx_sem, row_sem, *, tm):
    i = pl.program_id(0)
    n_steps = pl.num_programs(0)
    slot = i % 2

    def idx_copy(step, sl):
        return pltpu.make_async_copy(dest_hbm.at[step], idx_smem.at[sl], idx_sem.at[sl])

    def row_copy(sl, t, k, d):
        return pltpu.make_async_copy(yb_hbm.at[pl.ds(d, 1)], ybuf.at[sl, k, pl.ds(t, 1)], row_sem.at[sl])

    def issue(sl):
        def start(t, carry):
            row_copy(sl, t, 0, idx_smem[sl, 2 * t]).start(priority=0)
            row_copy(sl, t, 1, idx_smem[sl, 2 * t + 1]).start(priority=1)
            return carry

        lax.fori_loop(0, tm, start, 0, unroll=ROW_DMA_UNROLL)

    @pl.when(i == 0)
    def _():
        idx_copy(0, 0).start()
        idx_copy(0, 0).wait()
        issue(0)

        @pl.when(n_steps > 1)
        def _():
            idx_copy(1, 1).start()

    def step(sl):
        @pl.when(i + 1 < n_steps)
        def _():
            idx_copy(i + 1, 1 - sl).wait()
            issue(1 - sl)

        @pl.when(i + 2 < n_steps)
        def _():
            idx_copy(i + 2, sl).start()

        def wait(t, carry):
            row_copy(sl, 0, 0, 0).wait()
            row_copy(sl, 0, 1, 0).wait()
            return carry

        lax.fori_loop(0, tm, wait, 0, unroll=ROW_DMA_UNROLL)
        a0, b0 = _unpack_pair(ybuf[sl, 0])
        a1, b1 = _unpack_pair(ybuf[sl, 1])
        g0 = route_ref[:, 2:3]
        g1 = route_ref[:, 3:4]
        half = a0.shape[1]
        o_ref[:, :half] = x1_ref[:, :half] + (g0 * a0 + g1 * a1)
        o_ref[:, half:] = x1_ref[:, half:] + (g0 * b0 + g1 * b1)

    for sl in range(2):
        pl.when(slot == sl)(functools.partial(step, sl))


def _combine(dest2d, yb, x1, route, tm):
    n, d = x1.shape
    w = yb.shape[1]
    return pl.pallas_call(
        functools.partial(_combine_kernel, tm=tm),
        grid=(n // tm,),
        in_specs=[
            pl.BlockSpec(memory_space=pl.ANY),
            pl.BlockSpec(memory_space=pl.ANY),
            pl.BlockSpec((tm, d), lambda i: (i, 0)),
            pl.BlockSpec((tm, LANES), lambda i: (i, 0)),
        ],
        out_specs=pl.BlockSpec((tm, d), lambda i: (i, 0)),
        out_shape=jax.ShapeDtypeStruct((n, d), F32),
        scratch_shapes=[pltpu.SMEM((2, 2 * tm), I32), pltpu.VMEM((2, 2, tm, w), U32),
                        pltpu.SemaphoreType.DMA((2,)), pltpu.SemaphoreType.DMA((2,))],
        compiler_params=_cparams(1),
        name="moe_combine",
    )(dest2d, yb, x1, route)


def _routing_plan(route, n):
    e_flat = route[:, :2].astype(I32).reshape(-1)
    blk = RANK_BLOCK
    onehot = (e_flat[:, None] == jnp.arange(N_EXPERTS, dtype=I32)[None, :]).astype(BF16).reshape(-1, blk, N_EXPERTS)
    tri = (jnp.arange(blk)[:, None] > jnp.arange(blk)[None, :]).astype(BF16)
    within = jnp.einsum('ij,bjk->bik', tri, onehot, preferred_element_type=F32)
    blk_tot = jnp.sum(onehot.astype(F32), axis=1)
    blk_base = jnp.cumsum(blk_tot, axis=0) - blk_tot
    rank = jnp.sum((within + blk_base[:, None, :]) * onehot.astype(F32), axis=-1).reshape(-1).astype(I32)
    counts = jnp.sum(blk_tot, axis=0).astype(I32)
    padded = (counts + MOE_BLOCK - 1) // MOE_BLOCK * MOE_BLOCK
    pad_end = jnp.cumsum(padded)
    pad_start = pad_end - padded
    dest = pad_start[e_flat] + rank
    cap = 2 * n + N_EXPERTS * MOE_BLOCK
    n_blk = cap // MOE_BLOCK
    blk_pos = jnp.arange(n_blk, dtype=I32) * MOE_BLOCK
    blk_exp = jnp.minimum(jnp.sum((pad_end[None, :] <= blk_pos[:, None]).astype(I32), axis=1), N_EXPERTS - 1)
    n_used = (pad_end[-1:] // MOE_BLOCK).astype(I32)
    return dest.astype(I32), blk_exp, n_used, cap


def _layer(x, mem, g_mix, w_in, qk_gain, na_rpb, t5, g_mem, w_mem_kv, w_out, g_ffn, w_r1, b_r1, w_r2, b_r2, w1, w3, w2):
    b, s, d = x.shape
    n = b * s
    tm = PROJ_TILE
    scale = HEAD_DIM ** -0.5
    dils = tuple(dil for _, dil in DIL_PAIRS)

    o_qd, o_kd, o_vd, o_qm = 3 * NA_W, 3 * NA_W + 3 * DIL_W, 3 * NA_W + 6 * DIL_W, 3 * NA_W + 9 * DIL_W
    w_bf = w_in.astype(BF16)
    col_blocks = [w_bf[:, :3 * NA_W], w_bf[:, o_qm:o_qm + MEM_W]]
    for g in range(len(DIL_PAIRS)):
        for base in (o_qd, o_kd, o_vd):
            col_blocks.append(w_bf[:, base + g * DIL_W:base + (g + 1) * DIL_W])
    w_perm = jnp.concatenate(col_blocks, axis=1)
    tile = lambda v: jnp.tile(v.astype(F32), CHUNK // HEAD_DIM)
    ones = jnp.ones((CHUNK,), F32)
    qa_g, ka_g = tile(qk_gain[0, 0]) * scale, tile(qk_gain[0, 1])
    qd_g, kd_g = tile(qk_gain[1, 0]) * scale, tile(qk_gain[1, 1])
    qm_g, km_g = tile(qk_gain[2, 0]) * scale, tile(qk_gain[2, 1])
    main_w = 3 * NA_W + MEM_W
    plan = [(True, 0, 0, 1), (True, 0, CHUNK, 1), (True, 0, 2 * CHUNK, 1), (True, 0, 3 * CHUNK, 1),
            (False, 0, 4 * CHUNK, 1), (False, 0, 5 * CHUNK, 1), (True, 0, 6 * CHUNK, 1)]
    gains = [qa_g, qa_g, ka_g, ka_g, ones, ones, qm_g]
    for gi, dil in enumerate(dils):
        plan += [(True, 1 + gi, 0, dil), (True, 1 + gi, DIL_W, dil), (False, 1 + gi, 2 * DIL_W, dil)]
        gains += [qd_g, kd_g, ones]
    tpb = s // tm
    out_shapes = [jax.ShapeDtypeStruct((n, main_w), BF16)]
    out_specs = [pl.BlockSpec((tm, main_w), lambda i: (i, 0))]
    for dil in dils:
        if dil == 1:
            out_shapes.append(jax.ShapeDtypeStruct((n, 3 * DIL_W), BF16))
            out_specs.append(pl.BlockSpec((tm, 3 * DIL_W), lambda i: (i, 0)))
        else:
            out_shapes.append(jax.ShapeDtypeStruct((b, dil, s // dil, 3 * DIL_W), BF16))
            out_specs.append(pl.BlockSpec((1, dil, tm // dil, 3 * DIL_W), lambda i: (i // tpb, 0, i % tpb, 0)))
    pm, *dgs = _project(x.reshape(n, d), g_mix, w_perm, jnp.stack(gains), tuple(plan), out_shapes, out_specs, tm)

    m_rows = mem.shape[0] * mem.shape[1]
    tmm = min(tm, m_rows)
    kv_plan = ((True, 0, 0, 1), (False, 0, CHUNK, 1))
    kvm = _project(mem.reshape(m_rows, d), g_mem, w_mem_kv.astype(BF16), jnp.stack([km_g, ones]), kv_plan,
                   [jax.ShapeDtypeStruct((m_rows, 2 * MEM_W), BF16)],
                   [pl.BlockSpec((tmm, 2 * MEM_W), lambda i: (i, 0))], tmm)[0]
    kvm = kvm.reshape(mem.shape[0], mem.shape[1], 2 * MEM_W)

    out_na = _neighbourhood_attention(pm.reshape(b, s, main_w), _na_bias_table(na_rpb), b, s)

    t5g = t5.reshape(T5_BUCKETS, len(DIL_PAIRS), DIL_HEADS_PER_GROUP)
    dil_outs = []
    for gi, (dg, dil) in enumerate(zip(dgs, dils)):
        dg = dg.reshape(b, dil, s // dil, 3 * DIL_W)
        dil_outs.append(_dilated_attention(dg, _dil_band_table(t5g, gi, dil), b, dil, s // dil))

    w_r = jnp.zeros((d, LANES), F32).at[:, :N_GROUPS].set(w_r1.astype(F32)).at[:, N_GROUPS:N_GROUPS + N_EXPERTS].set(w_r2.astype(F32))
    b_r = jnp.zeros((1, LANES), F32).at[0, :N_GROUPS].set(b_r1.astype(F32)).at[0, N_GROUPS:N_GROUPS + N_EXPERTS].set(b_r2.astype(F32))
    x1, h_packed, route = _mix(x.reshape(n, d), out_na.reshape(n, NA_W), pm, 3 * NA_W, dil_outs, kvm,
                               w_out.astype(BF16), g_ffn, w_r, b_r, s)

    dest, blk_exp, n_used, cap = _routing_plan(route, n)
    tmd = MOE_ROW_TILE
    dest2d = dest.reshape(n // tmd, 2 * tmd)
    xs = _dispatch(dest2d, h_packed, cap, tmd)
    yb = _experts(blk_exp, n_used, xs, w1.astype(BF16), w3.astype(BF16), w2.astype(BF16))
    out = _combine(dest2d, yb, x1, route, tmd)
    return out.reshape(b, s, d)


def kernel(x, mem, g_mix, w_in, qk_gain, na_rpb, t5_table, g_mem, w_mem_kv, w_out, g_ffn, w_r1, b_r1, w_r2, b_r2, w1, w3, w2):
    for l in range(g_mix.shape[0]):
        x = _layer(x, mem, g_mix[l], w_in[l], qk_gain[l], na_rpb[l], t5_table, g_mem[l], w_mem_kv[l], w_out[l],
                   g_ffn[l], w_r1[l], b_r1[l], w_r2[l], b_r2[l], w1[l], w3[l], w2[l])
    return x
```
